```python
import math
import jax, jax.numpy as jnp
from jax import lax
import numpy as np

D_MODEL = 1024
BATCH = 8
SEQ = 8192
DEPTH = 1
DEC_BATCH = 1
DEC_SEQ = 16384
PAST_LEN = 128

FNET_GROUPS = 4
FNET_GROUP_DIM = 128
FNET_WIDTH = FNET_GROUPS * FNET_GROUP_DIM
SSM_GROUP_DIM = 16
SSM_GROUPS = 32
SSM_WIDTH = SSM_GROUPS * SSM_GROUP_DIM
SSM_STATE = 64
N_DIR = 2
DT_MIN = 1e-3
DT_MAX = 1e-1
IN_WIDTH = FNET_WIDTH + SSM_WIDTH + 2 * D_MODEL
FFN_HIDDEN = int(math.ceil(8 * D_MODEL / 3 / 256) * 256)
EPS = 1e-6

kernel_name = "hybrid_fnet_s5_gated_encoder"


def rms_norm(x, g):
    xf = x.astype(jnp.float32)
    y = xf * lax.rsqrt(jnp.mean(xf * xf, axis=-1, keepdims=True) + EPS)
    return (y * g.astype(jnp.float32)).astype(x.dtype)


def fourier_mix(u):
    b, s, _ = u.shape
    ug = u.astype(jnp.float32).reshape(b, s, FNET_GROUPS, FNET_GROUP_DIM)
    f = jnp.fft.fftn(ug, axes=(1, 3), norm="ortho").real
    return f.reshape(b, s, FNET_WIDTH).astype(u.dtype)


def _scan_op(left, right):
    a_l, b_l = left
    a_r, b_r = right
    return a_r * a_l, a_r * b_l + b_r


def ssm_direction(u, lam_re, lam_im, log_dt, b_re, b_im, c_re, c_im, reverse):
    f32 = jnp.float32
    lam = lax.complex(lam_re.astype(f32), lam_im.astype(f32))
    dt = jnp.exp(log_dt.astype(f32))[:, None]
    lam_bar = jnp.exp(lam * dt)
    b = lax.complex(b_re.astype(f32), b_im.astype(f32))
    b_bar = ((lam_bar - 1.0) / lam)[:, :, None] * b
    bu = jnp.einsum('sgh,gph->sgp', u.astype(jnp.complex64), b_bar)
    a = jnp.broadcast_to(lam_bar, bu.shape)
    _, states = lax.associative_scan(_scan_op, (a, bu), reverse=reverse, axis=0)
    c = lax.complex(c_re.astype(f32), c_im.astype(f32))
    return jnp.einsum('sgp,ghp->sgh', states, c).real


def ssm_sequence(u, lam_re, lam_im, log_dt, b_re, b_im, c_re, c_im, d_skip):
    s = u.shape[0]
    ug = u.astype(jnp.float32).reshape(s, SSM_GROUPS, SSM_GROUP_DIM)
    y_fwd = ssm_direction(ug, lam_re[0], lam_im[0], log_dt[0], b_re[0], b_im[0], c_re[0], c_im[0], False)
    y_bwd = ssm_direction(ug, lam_re[1], lam_im[1], log_dt[1], b_re[1], b_im[1], c_re[1], c_im[1], True)
    y = y_fwd + y_bwd + d_skip.astype(jnp.float32).reshape(SSM_GROUPS, SSM_GROUP_DIM) * ug
    return y.reshape(s, SSM_WIDTH).astype(u.dtype)


def mixer(h, w_in, w_fnet_out, lam_re, lam_im, log_dt, b_re, b_im, c_re, c_im, d_skip,
          w_glu_val, w_glu_gate, w_out):
    proj = jnp.einsum('bsd,de->bse', h, w_in)
    o1 = FNET_WIDTH
    o2 = o1 + SSM_WIDTH
    o3 = o2 + D_MODEL
    u_f, u_s, g_a, g_b = proj[..., :o1], proj[..., o1:o2], proj[..., o2:o3], proj[..., o3:]
    br_a = jnp.einsum('bsf,fd->bsd', fourier_mix(u_f), w_fnet_out)
    y_s = lax.map(lambda u: ssm_sequence(u, lam_re, lam_im, log_dt, b_re, b_im, c_re, c_im, d_skip), u_s)
    z = jax.nn.gelu(y_s)
    br_b = jnp.einsum('bsf,fd->bsd', z, w_glu_val) * jax.nn.sigmoid(jnp.einsum('bsf,fd->bsd', z, w_glu_gate))
    merged = jax.nn.sigmoid(g_a) * br_a + jax.nn.sigmoid(g_b) * br_b
    return jnp.einsum('bsd,de->bse', merged, w_out)


def swiglu(h, w_gate, w_up, w_down):
    g = jnp.einsum('bsd,df->bsf', h, w_gate)
    u = jnp.einsum('bsd,df->bsf', h, w_up)
    return jnp.einsum('bsf,fd->bsd', jax.nn.silu(g) * u, w_down)


def encoder_stack(x, norm_mix_pre, norm_mix_post, norm_ffn_pre, norm_ffn_post, w_in, w_fnet_out,
                  lam_re, lam_im, log_dt, b_re, b_im, c_re, c_im, d_skip, w_glu_val, w_glu_gate,
                  w_out, w_ffn_gate, w_ffn_up, w_ffn_down):
    for l in range(DEPTH):
        h = rms_norm(x, norm_mix_pre[l])
        m = mixer(h, w_in[l], w_fnet_out[l], lam_re[l], lam_im[l], log_dt[l], b_re[l], b_im[l],
                  c_re[l], c_im[l], d_skip[l], w_glu_val[l], w_glu_gate[l], w_out[l])
        x = x + rms_norm(m, norm_mix_post[l])
        h = rms_norm(x, norm_ffn_pre[l])
        f = swiglu(h, w_ffn_gate[l], w_ffn_up[l], w_ffn_down[l])
        x = x + rms_norm(f, norm_ffn_post[l])
    return x


def setup_inputs(seed: int = 0) -> dict:
    key = jax.random.key(seed)
    ks = jax.random.split(key, 24)
    f32 = jnp.float32
    L = DEPTH

    def nrm(k, shape, fan_in):
        return jax.random.normal(k, shape, f32) * (fan_in ** -0.5)

    def gain(k):
        return 1.0 + 0.02 * jax.random.normal(k, (L, D_MODEL), f32)

    n_idx = jnp.arange(SSM_STATE, dtype=f32)
    lam_re = -0.5 + 0.01 * jax.random.normal(ks[6], (L, N_DIR, SSM_GROUPS, SSM_STATE), f32)
    lam_im = math.pi * n_idx + 0.01 * jax.random.normal(ks[7], (L, N_DIR, SSM_GROUPS, SSM_STATE), f32)
    log_dt = jax.random.uniform(ks[8], (L, N_DIR, SSM_GROUPS), f32,
                                minval=math.log(DT_MIN), maxval=math.log(DT_MAX))
    return {
        "x_prompt": jax.random.normal(ks[0], (BATCH, SEQ, D_MODEL), f32),
        "x_sample": jax.random.normal(ks[1], (DEC_BATCH, DEC_SEQ, D_MODEL), f32),
        "norm_mix_pre": gain(ks[2]),
        "norm_mix_post": gain(ks[3]),
        "norm_ffn_pre": gain(ks[4]),
        "norm_ffn_post": gain(ks[5]),
        "w_in": nrm(ks[9], (L, D_MODEL, IN_WIDTH), D_MODEL),
        "w_fnet_out": nrm(ks[10], (L, FNET_WIDTH, D_MODEL), FNET_WIDTH),
        "lam_re": lam_re,
        "lam_im": lam_im,
        "log_dt": log_dt,
        "b_re": nrm(ks[11], (L, N_DIR, SSM_GROUPS, SSM_STATE, SSM_GROUP_DIM), 2 * SSM_GROUP_DIM),
        "b_im": nrm(ks[12], (L, N_DIR, SSM_GROUPS, SSM_STATE, SSM_GROUP_DIM), 2 * SSM_GROUP_DIM),
        "c_re": nrm(ks[13], (L, N_DIR, SSM_GROUPS, SSM_GROUP_DIM, SSM_STATE), 2 * SSM_STATE),
        "c_im": nrm(ks[14], (L, N_DIR, SSM_GROUPS, SSM_GROUP_DIM, SSM_STATE), 2 * SSM_STATE),
        "d_skip": 1.0 + 0.1 * jax.random.normal(ks[15], (L, SSM_WIDTH), f32),
        "w_glu_val": nrm(ks[16], (L, SSM_WIDTH, D_MODEL), SSM_WIDTH),
        "w_glu_gate": nrm(ks[17], (L, SSM_WIDTH, D_MODEL), SSM_WIDTH),
        "w_out": nrm(ks[18], (L, D_MODEL, D_MODEL), D_MODEL),
        "w_ffn_gate": nrm(ks[19], (L, D_MODEL, FFN_HIDDEN), D_MODEL),
        "w_ffn_up": nrm(ks[20], (L, D_MODEL, FFN_HIDDEN), D_MODEL),
        "w_ffn_down": nrm(ks[21], (L, FFN_HIDDEN, D_MODEL), FFN_HIDDEN),
    }


def reference(x_prompt, x_sample, norm_mix_pre, norm_mix_post, norm_ffn_pre, norm_ffn_post, w_in,
              w_fnet_out, lam_re, lam_im, log_dt, b_re, b_im, c_re, c_im, d_skip, w_glu_val,
              w_glu_gate, w_out, w_ffn_gate, w_ffn_up, w_ffn_down):
    y_prompt = encoder_stack(x_prompt, norm_mix_pre, norm_mix_post, norm_ffn_pre, norm_ffn_post, w_in,
                             w_fnet_out, lam_re, lam_im, log_dt, b_re, b_im, c_re, c_im, d_skip,
                             w_glu_val, w_glu_gate, w_out, w_ffn_gate, w_ffn_up, w_ffn_down)
    y_sample = encoder_stack(x_sample, norm_mix_pre, norm_mix_post, norm_ffn_pre, norm_ffn_post, w_in,
                             w_fnet_out, lam_re, lam_im, log_dt, b_re, b_im, c_re, c_im, d_skip,
                             w_glu_val, w_glu_gate, w_out, w_ffn_gate, w_ffn_up, w_ffn_down)
    return (y_prompt, y_sample)
```

```python
import functools
import math

import numpy as np
import jax
import jax.numpy as jnp
from jax import lax
from jax.experimental import pallas as pl
from jax.experimental.pallas import tpu as pltpu

F32 = jnp.float32
BF16 = jnp.bfloat16

EPS = 1e-6
SUBLANES = 8
LANES = 128
NB = SUBLANES

FNET_GROUP_DIM = 128
SSM_GROUP_DIM = 16
SSM_STATE = 64
GROUPS_PER_BLOCK = LANES // SSM_GROUP_DIM
STATES_PER_BLOCK = GROUPS_PER_BLOCK * SSM_STATE

FFT_N2 = 128
VMEM_LIMIT = 56 * 1024 * 1024


def _dot(a, b):
    return jnp.dot(a, b, preferred_element_type=F32)


def _rms(x, g):
    return x * lax.rsqrt(jnp.mean(x * x, axis=-1, keepdims=True) + EPS) * g


def _const_spec(shape):
    nd = len(shape)
    return pl.BlockSpec(shape, lambda *_: (0,) * nd, pipeline_mode=pl.Buffered(1))


def _in_proj_kernel(x_ref, g_ref, w_ref, uf_ref, us_ref, sa_ref, sb_ref, *, tt, fw, sw, d):
    rows = NB * tt
    x = x_ref[...].reshape(rows, d)
    h = _rms(x, g_ref[...]).astype(BF16)
    uf_ref[...] = _dot(h, w_ref[:, 0:fw]).reshape(NB, tt, fw).astype(BF16)
    us = _dot(h, w_ref[:, fw:fw + sw])
    for cb in range(sw // LANES):
        for b in range(NB):
            us_ref[cb, pl.ds(b, tt, stride=NB), :] = us[b * tt:(b + 1) * tt, cb * LANES:(cb + 1) * LANES]
    o = fw + sw
    sa_ref[...] = jax.nn.sigmoid(_dot(h, w_ref[:, o:o + d])).reshape(NB, tt, d).astype(BF16)
    sb_ref[...] = jax.nn.sigmoid(_dot(h, w_ref[:, o + d:o + 2 * d])).reshape(NB, tt, d).astype(BF16)


def _in_proj(xv, gain, w_in, fw, sw, tt):
    nb, sseg, d = xv.shape
    n = sseg // tt
    kern = functools.partial(_in_proj_kernel, tt=tt, fw=fw, sw=sw, d=d)
    return pl.pallas_call(
        kern,
        grid=(n,),
        in_specs=[
            pl.BlockSpec((NB, tt, d), lambda i: (0, i, 0)),
            _const_spec((1, d)),
            _const_spec(w_in.shape),
        ],
        out_specs=[
            pl.BlockSpec((NB, tt, fw), lambda i: (0, i, 0)),
            pl.BlockSpec((sw // LANES, NB * tt, LANES), lambda i: (0, i, 0)),
            pl.BlockSpec((NB, tt, d), lambda i: (0, i, 0)),
            pl.BlockSpec((NB, tt, d), lambda i: (0, i, 0)),
        ],
        out_shape=[
            jax.ShapeDtypeStruct((NB, sseg, fw), BF16),
            jax.ShapeDtypeStruct((sw // LANES, sseg * NB, LANES), F32),
            jax.ShapeDtypeStruct((NB, sseg, d), BF16),
            jax.ShapeDtypeStruct((NB, sseg, d), BF16),
        ],
        compiler_params=pltpu.CompilerParams(
            dimension_semantics=("arbitrary",), vmem_limit_bytes=VMEM_LIMIT),
        name="in_proj",
    )(xv, gain, w_in)


def _fft_tables(s):
    n2 = FFT_N2
    n1 = s // n2
    s1 = np.arange(n1, dtype=np.int64)
    k1 = np.arange(n1, dtype=np.int64)
    s2 = np.arange(n2, dtype=np.int64)
    idx = (k1[None, :, None] * s1[None, None, :] * n2 + s2[:, None, None] * k1[None, :, None]) % s
    ang = 2.0 * np.pi * idx.astype(np.float64) / s
    g = np.stack([np.cos(ang), -np.sin(ang)], axis=2) / math.sqrt(n1)
    g = g.reshape(n2, 2 * n1, n1)
    k2 = np.arange(n2, dtype=np.int64)
    ang2 = 2.0 * np.pi * ((k2[:, None] * s2[None, :]) % n2).astype(np.float64) / n2
    c2, s2m = np.cos(ang2), np.sin(ang2)
    h = np.block([[c2, s2m], [-s2m, c2]]) / math.sqrt(n2)
    c = np.arange(FNET_GROUP_DIM, dtype=np.int64)
    angc = 2.0 * np.pi * ((c[:, None] * c[None, :]) % FNET_GROUP_DIM).astype(np.float64) / FNET_GROUP_DIM
    cs = np.concatenate([np.cos(angc), np.sin(angc)], axis=0) / math.sqrt(FNET_GROUP_DIM)
    return g.astype(np.float32), h.astype(np.float32), cs.astype(np.float32)


def _fft1_kernel(x_ref, g_ref, o_ref, *, s2blk, c):
    for j in range(s2blk):
        o_ref[:, j * c:(j + 1) * c] = _dot(g_ref[j], x_ref[:, j * c:(j + 1) * c]).astype(BF16)


def _fft2_kernel(a_ref, h_ref, cs_ref, o_ref, *, k1blk, c, n2):
    gd = FNET_GROUP_DIM
    for j in range(k1blk):
        a = a_ref[j].reshape(2 * n2, c)
        y = _dot(h_ref[...], a).astype(BF16)
        for g in range(c // gd):
            yg = jnp.concatenate(
                [y[0:n2, g * gd:(g + 1) * gd], y[n2:2 * n2, g * gd:(g + 1) * gd]], axis=1)
            o_ref[:, j * c + g * gd:j * c + (g + 1) * gd] = _dot(yg, cs_ref[...]).astype(BF16)


def _fourier_mix(u, s2blk=16, k1blk=8):
    b, s, c = u.shape
    n2 = FFT_N2
    n1 = s // n2
    assert n1 * n2 == s and n1 % k1blk == 0 and n2 % s2blk == 0
    g_np, h_np, cs_np = _fft_tables(s)
    g_t = jnp.asarray(g_np).astype(BF16)
    h_t = jnp.asarray(h_np).astype(BF16)
    cs_t = jnp.asarray(cs_np).astype(BF16)

    a = pl.pallas_call(
        functools.partial(_fft1_kernel, s2blk=s2blk, c=c),
        grid=(b, n2 // s2blk),
        in_specs=[
            pl.BlockSpec((None, n1, s2blk * c), lambda bi, i: (bi, 0, i)),
            pl.BlockSpec((s2blk, 2 * n1, n1), lambda bi, i: (i, 0, 0)),
        ],
        out_specs=pl.BlockSpec((None, 2 * n1, s2blk * c), lambda bi, i: (bi, 0, i)),
        out_shape=jax.ShapeDtypeStruct((b, 2 * n1, n2 * c), BF16),
        compiler_params=pltpu.CompilerParams(
            dimension_semantics=("arbitrary", "arbitrary"), vmem_limit_bytes=VMEM_LIMIT),
        name="fft_stage1",
    )(u.reshape(b, n1, n2 * c), g_t)

    f = pl.pallas_call(
        functools.partial(_fft2_kernel, k1blk=k1blk, c=c, n2=n2),
        grid=(b, n1 // k1blk),
        in_specs=[
            pl.BlockSpec((None, k1blk, 2, n2, c), lambda bi, i: (bi, i, 0, 0, 0)),
            _const_spec((2 * n2, 2 * n2)),
            _const_spec((2 * FNET_GROUP_DIM, FNET_GROUP_DIM)),
        ],
        out_specs=pl.BlockSpec((None, n2, k1blk * c), lambda bi, i: (bi, 0, i)),
        out_shape=jax.ShapeDtypeStruct((b, n2, n1 * c), BF16),
        compiler_params=pltpu.CompilerParams(
            dimension_semantics=("arbitrary", "arbitrary"), vmem_limit_bytes=VMEM_LIMIT),
        name="fft_stage2",
    )(a.reshape(b, n1, 2, n2, c), h_t, cs_t)
    return f.reshape(b, s, c)


def _ssm_tables(lam_re, lam_im, log_dt, b_re, b_im, c_re, c_im, seg_len):
    ndir, ngroups, p = lam_re.shape
    hdim = b_re.shape[-1]
    nblk = ngroups // GROUPS_PER_BLOCK
    lam_re = lam_re.astype(F32)
    lam_im = lam_im.astype(F32)
    dt = jnp.exp(log_dt.astype(F32))[..., None]
    mag = jnp.exp(lam_re * dt)
    ar = mag * jnp.cos(lam_im * dt)
    ai = mag * jnp.sin(lam_im * dt)
    den = lam_re * lam_re + lam_im * lam_im
    qr = ((ar - 1.0) * lam_re + ai * lam_im) / den
    qi = (ai * lam_re - (ar - 1.0) * lam_im) / den
    bre = qr[..., None] * b_re.astype(F32) - qi[..., None] * b_im.astype(F32)
    bim = qr[..., None] * b_im.astype(F32) + qi[..., None] * b_re.astype(F32)
    magl = jnp.exp(lam_re * dt * seg_len)
    alr = magl * jnp.cos(lam_im * dt * seg_len)
    ali = magl * jnp.sin(lam_im * dt * seg_len)

    eye = jnp.eye(GROUPS_PER_BLOCK, dtype=F32)

    def blockdiag_in(w):
        w = w.reshape(ndir, nblk, GROUPS_PER_BLOCK, p, hdim)
        m = jnp.einsum('dkgph,gj->dkghjp', w, eye)
        return m.reshape(ndir, nblk, GROUPS_PER_BLOCK * hdim, GROUPS_PER_BLOCK * p)

    def blockdiag_out(w):
        w = w.reshape(ndir, nblk, GROUPS_PER_BLOCK, hdim, p)
        m = jnp.einsum('dkghp,gj->dkgpjh', w, eye)
        return m.reshape(ndir, nblk, GROUPS_PER_BLOCK * p, GROUPS_PER_BLOCK * hdim)

    bcat = jnp.concatenate([blockdiag_in(bre), blockdiag_in(bim)], axis=-1).astype(BF16)
    ccat = jnp.concatenate([blockdiag_out(c_re.astype(F32)),
                            -blockdiag_out(c_im.astype(F32))], axis=-2).astype(BF16)

    def lanes(v):
        v = v.reshape(ndir, nblk, 1, STATES_PER_BLOCK)
        return jnp.broadcast_to(v, (ndir, nblk, SUBLANES, STATES_PER_BLOCK))

    return bcat, ccat, lanes(ar), lanes(ai), lanes(alr), lanes(ali)


def _ssm_kernel(*refs, t_chunk, nblk, chain, need_y):
    if need_y:
        (uf_ref, ub_ref, e_ref, bcat_ref, ccat_ref, are_ref, aim_ref, alre_ref, alim_ref, dsk_ref,
         yf_ref, yb_ref, eo_ref, s_ref, st_ref) = refs
    else:
        (uf_ref, ub_ref, e_ref, bcat_ref, ccat_ref, are_ref, aim_ref, alre_ref, alim_ref, dsk_ref,
         eo_ref, s_ref, st_ref) = refs
        yf_ref = yb_ref = None
    i = pl.program_id(0)
    n = pl.num_programs(0)
    sp = STATES_PER_BLOCK

    @pl.when(i == 0)
    def _init():
        if not chain:
            st_ref[...] = jnp.zeros(st_ref.shape, F32)
        else:
            row = lax.broadcasted_iota(jnp.int32, (SUBLANES, sp), 0)
            for d in range(2):
                shift = 1 if d == 0 else SUBLANES - 1
                keep = (row >= 1) if d == 0 else (row <= SUBLANES - 2)
                for cb in range(nblk):
                    er = pltpu.roll(e_ref[d, cb, :, 0:sp], shift, 0)
                    ei = pltpu.roll(e_ref[d, cb, :, sp:2 * sp], shift, 0)
                    lr = alre_ref[d, cb]
                    li = alim_ref[d, cb]
                    xr = jnp.zeros((SUBLANES, sp), F32)
                    xi = jnp.zeros((SUBLANES, sp), F32)
                    for _ in range(SUBLANES - 1):
                        pr = pltpu.roll(xr, shift, 0)
                        pi = pltpu.roll(xi, shift, 0)
                        xr = jnp.where(keep, lr * pr - li * pi + er, 0.0)
                        xi = jnp.where(keep, lr * pi + li * pr + ei, 0.0)
                    st_ref[d, cb, :, 0:sp] = xr
                    st_ref[d, cb, :, sp:2 * sp] = xi

    for d in range(2):
        u_ref = uf_ref if d == 0 else ub_ref
        y_ref = yf_ref if d == 0 else yb_ref
        for cb in range(nblk):
            u = u_ref[cb]
            s_ref[...] = _dot(u.astype(BF16), bcat_ref[d, cb])
            a_re = are_ref[d, cb]
            a_im = aim_ref[d, cb]

            def body(k, carry, d=d, a_re=a_re, a_im=a_im):
                re, im = carry
                t = k if d == 0 else t_chunk - 1 - k
                r = pl.multiple_of(t * SUBLANES, SUBLANES)
                n_re = a_re * re - a_im * im + s_ref[pl.ds(r, SUBLANES), 0:sp]
                n_im = a_re * im + a_im * re + s_ref[pl.ds(r, SUBLANES), sp:2 * sp]
                s_ref[pl.ds(r, SUBLANES), 0:sp] = n_re
                s_ref[pl.ds(r, SUBLANES), sp:2 * sp] = n_im
                return n_re, n_im

            re, im = lax.fori_loop(
                0, t_chunk, body, (st_ref[d, cb, :, 0:sp], st_ref[d, cb, :, sp:2 * sp]), unroll=4)
            st_ref[d, cb, :, 0:sp] = re
            st_ref[d, cb, :, sp:2 * sp] = im
            if need_y:
                y = _dot(s_ref[...].astype(BF16), ccat_ref[d, cb])
                if d == 0:
                    y = y + dsk_ref[:, cb * LANES:(cb + 1) * LANES] * u
                y_ref[cb] = y

    @pl.when(i == n - 1)
    def _fin():
        eo_ref[...] = st_ref[...]


def _ssm(us, e_in, tables, d_skip, t_chunk, chain, need_y):
    bcat, ccat, are, aim, alre, alim = tables
    nblk, rows_total, _ = us.shape
    w = nblk * LANES
    sseg = rows_total // NB
    n = sseg // t_chunk
    rows = t_chunk * NB
    st_shape = (2, nblk, SUBLANES, 2 * STATES_PER_BLOCK)
    kern = functools.partial(_ssm_kernel, t_chunk=t_chunk, nblk=nblk, chain=chain, need_y=need_y)
    out_specs = [_const_spec(st_shape)]
    out_shape = [jax.ShapeDtypeStruct(st_shape, F32)]
    if need_y:
        out_specs = [pl.BlockSpec((nblk, rows, LANES), lambda i: (0, i, 0)),
                     pl.BlockSpec((nblk, rows, LANES), lambda i: (0, n - 1 - i, 0))] + out_specs
        out_shape = [jax.ShapeDtypeStruct((nblk, rows_total, LANES), F32),
                     jax.ShapeDtypeStruct((nblk, rows_total, LANES), F32)] + out_shape
    return pl.pallas_call(
        kern,
        grid=(n,),
        in_specs=[
            pl.BlockSpec((nblk, rows, LANES), lambda i: (0, i, 0)),
            pl.BlockSpec((nblk, rows, LANES), lambda i: (0, n - 1 - i, 0)),
            _const_spec(st_shape),
            _const_spec(bcat.shape),
            _const_spec(ccat.shape),
            _const_spec(are.shape),
            _const_spec(aim.shape),
            _const_spec(alre.shape),
            _const_spec(alim.shape),
            _const_spec((1, w)),
        ],
        out_specs=out_specs,
        out_shape=out_shape,
        scratch_shapes=[
            pltpu.VMEM((rows, 2 * STATES_PER_BLOCK), F32),
            pltpu.VMEM(st_shape, F32),
        ],
        compiler_params=pltpu.CompilerParams(
            dimension_semantics=("arbitrary",), vmem_limit_bytes=VMEM_LIMIT),
        name="ssm_scan_y" if need_y else "ssm_scan_states",
    )(us, us, e_in, bcat, ccat, are, aim, alre, alim, d_skip)


def _ffn_chunks(hidden, step=1024):
    edges = list(range(0, hidden, step)) + [hidden]
    return list(zip(edges[:-1], edges[1:]))


def _mix_ffn_kernel(x_ref, f_ref, yf_ref, yb_ref, sa_ref, sb_ref, gpost_ref, gpre_ref, gfpost_ref,
                    wfo_ref, wval_ref, wgate_ref, wout_ref, wfg_ref, wfu_ref, wfd_ref, o_ref,
                    *, tt, d, fw, sw, hidden):
    rows = NB * tt
    ys = jnp.concatenate(
        [jnp.concatenate(
            [yf_ref[cb, pl.ds(b, tt, stride=NB), :] + yb_ref[cb, pl.ds(b, tt, stride=NB), :]
             for b in range(NB)], axis=0)
         for cb in range(sw // LANES)], axis=1)
    z = jax.nn.gelu(ys).astype(BF16)
    br_a = _dot(f_ref[...].reshape(rows, fw), wfo_ref[...])
    br_b = _dot(z, wval_ref[...]) * jax.nn.sigmoid(_dot(z, wgate_ref[...]))
    sa = sa_ref[...].reshape(rows, d).astype(F32)
    sb = sb_ref[...].reshape(rows, d).astype(F32)
    merged = (sa * br_a + sb * br_b).astype(BF16)
    m = _dot(merged, wout_ref[...])
    x1 = x_ref[...].reshape(rows, d) + _rms(m, gpost_ref[...])
    h2 = _rms(x1, gpre_ref[...]).astype(BF16)
    acc = None
    for c0, c1 in _ffn_chunks(hidden):
        g = _dot(h2, wfg_ref[:, c0:c1])
        u = _dot(h2, wfu_ref[:, c0:c1])
        part = _dot((jax.nn.silu(g) * u).astype(BF16), wfd_ref[c0:c1, :])
        acc = part if acc is None else acc + part
    o_ref[...] = (x1 + _rms(acc, gfpost_ref[...])).reshape(NB, tt, d)


def _mix_ffn(xv, f, yf, yb, sa, sb, gpost, gpre, gfpost, wfo, wval, wgate, wout, wfg, wfu, wfd, tt):
    nb, sseg, d = xv.shape
    fw = f.shape[-1]
    nblk = yf.shape[0]
    sw = nblk * LANES
    hidden = wfg.shape[-1]
    n = sseg // tt
    kern = functools.partial(_mix_ffn_kernel, tt=tt, d=d, fw=fw, sw=sw, hidden=hidden)
    tok = lambda width: pl.BlockSpec((NB, tt, width), lambda i: (0, i, 0))
    tmaj = pl.BlockSpec((nblk, NB * tt, LANES), lambda i: (0, i, 0))
    return pl.pallas_call(
        kern,
        grid=(n,),
        in_specs=[tok(d), tok(fw), tmaj, tmaj, tok(d), tok(d),
                  _const_spec((1, d)), _const_spec((1, d)), _const_spec((1, d)),
                  _const_spec(wfo.shape), _const_spec(wval.shape), _const_spec(wgate.shape),
                  _const_spec(wout.shape), _const_spec(wfg.shape), _const_spec(wfu.shape),
                  _const_spec(wfd.shape)],
        out_specs=tok(d),
        out_shape=jax.ShapeDtypeStruct((NB, sseg, d), F32),
        compiler_params=pltpu.CompilerParams(
            dimension_semantics=("arbitrary",), vmem_limit_bytes=VMEM_LIMIT),
        name="mix_ffn",
    )(xv, f, yf, yb, sa, sb, gpost, gpre, gfpost, wfo, wval, wgate, wout, wfg, wfu, wfd)


def _pick(total, want):
    t = min(want, total)
    assert total % t == 0, (total, t)
    return t


def _encoder_layer(x, p):
    b, s, d = x.shape
    assert b in (1, NB), "one sequence (split in NB segments) or NB sequences"
    sseg = (b * s) // NB
    fw = p["w_fnet_out"].shape[0]
    sw = p["w_glu_val"].shape[0]
    xv = x.reshape(NB, sseg, d)
    tt = _pick(sseg, 64)

    uf, us, sa, sb = _in_proj(xv, p["norm_mix_pre"], p["w_in"], fw, sw, tt)
    f = _fourier_mix(uf.reshape(b, s, fw)).reshape(NB, sseg, fw)

    tables = _ssm_tables(p["lam_re"], p["lam_im"], p["log_dt"], p["b_re"], p["b_im"],
                         p["c_re"], p["c_im"], sseg)
    t_chunk = _pick(sseg, 128)
    nblk = sw // LANES
    e0 = jnp.zeros((2, nblk, SUBLANES, 2 * STATES_PER_BLOCK), F32)
    if b == NB:
        yf, yb, _ = _ssm(us, e0, tables, p["d_skip"], t_chunk, chain=False, need_y=True)
    else:
        (e,) = _ssm(us, e0, tables, p["d_skip"], t_chunk, chain=False, need_y=False)
        yf, yb, _ = _ssm(us, e, tables, p["d_skip"], t_chunk, chain=True, need_y=True)

    y = _mix_ffn(xv, f, yf, yb, sa, sb, p["norm_mix_post"], p["norm_ffn_pre"], p["norm_ffn_post"],
                 p["w_fnet_out"], p["w_glu_val"], p["w_glu_gate"], p["w_out"],
                 p["w_ffn_gate"], p["w_ffn_up"], p["w_ffn_down"], tt)
    return y.reshape(b, s, d)


def _layer_params(l, norm_mix_pre, norm_mix_post, norm_ffn_pre, norm_ffn_post, w_in, w_fnet_out,
                  lam_re, lam_im, log_dt, b_re, b_im, c_re, c_im, d_skip, w_glu_val, w_glu_gate,
                  w_out, w_ffn_gate, w_ffn_up, w_ffn_down):
    row = lambda v: v[l].astype(F32).reshape(1, -1)
    w = lambda v: v[l].astype(BF16)
    return dict(
        norm_mix_pre=row(norm_mix_pre), norm_mix_post=row(norm_mix_post),
        norm_ffn_pre=row(norm_ffn_pre), norm_ffn_post=row(norm_ffn_post),
        w_in=w(w_in), w_fnet_out=w(w_fnet_out),
        lam_re=lam_re[l], lam_im=lam_im[l], log_dt=log_dt[l],
        b_re=b_re[l], b_im=b_im[l], c_re=c_re[l], c_im=c_im[l], d_skip=row(d_skip),
        w_glu_val=w(w_glu_val), w_glu_gate=w(w_glu_gate), w_out=w(w_out),
        w_ffn_gate=w(w_ffn_gate), w_ffn_up=w(w_ffn_up), w_ffn_down=w(w_ffn_down))


def kernel(x_prompt, x_sample, norm_mix_pre, norm_mix_post, norm_ffn_pre, norm_ffn_post, w_in, w_fnet_out, lam_re, lam_im, log_dt, b_re, b_im, c_re, c_im, d_skip, w_glu_val, w_glu_gate, w_out, w_ffn_gate, w_ffn_up, w_ffn_down):
    weights = (norm_mix_pre, norm_mix_post, norm_ffn_pre, norm_ffn_post, w_in, w_fnet_out,
               lam_re, lam_im, log_dt, b_re, b_im, c_re, c_im, d_skip, w_glu_val, w_glu_gate,
               w_out, w_ffn_gate, w_ffn_up, w_ffn_down)
    depth = w_in.shape[0]
    outs = []
    for x in (x_prompt, x_sample):
        for l in range(depth):
            x = _encoder_layer(x, _layer_params(l, *weights))
        outs.append(x)
    return tuple(outs)
```

```python
import functools
import math

import numpy as np
import jax
import jax.numpy as jnp
from jax import lax
from jax.experimental import pallas as pl
from jax.experimental.pallas import tpu as pltpu

F32 = jnp.float32
BF16 = jnp.bfloat16

EPS = 1e-6
SUBLANES = 8
LANES = 128
NB = SUBLANES

FNET_GROUP_DIM = 128
SSM_GROUP_DIM = 16
SSM_STATE = 64
GROUPS_PER_BLOCK = LANES // SSM_GROUP_DIM
STATES_PER_BLOCK = GROUPS_PER_BLOCK * SSM_STATE

FFT_N2 = 128
VMEM_LIMIT = 56 * 1024 * 1024


def _dot(a, b):
    return jnp.dot(a, b, preferred_element_type=F32)


def _rms(x, g):
    return x * lax.rsqrt(jnp.mean(x * x, axis=-1, keepdims=True) + EPS) * g


def _const_spec(shape):
    nd = len(shape)
    return pl.BlockSpec(shape, lambda *_: (0,) * nd, pipeline_mode=pl.Buffered(1))


def _in_proj_kernel(x_ref, g_ref, w_ref, uf_ref, us_ref, sa_ref, sb_ref, *, tt, fw, sw, d):
    rows = NB * tt
    x = x_ref[...].reshape(rows, d)
    h = _rms(x, g_ref[...]).astype(BF16)
    uf = _dot(h, w_ref[:, 0:fw])
    for g in range(fw // LANES):
        uf_ref[g] = uf[:, g * LANES:(g + 1) * LANES].reshape(NB, tt, LANES)
    us = _dot(h, w_ref[:, fw:fw + sw])
    for cb in range(sw // LANES):
        for b in range(NB):
            us_ref[cb, pl.ds(b, tt, stride=NB), :] = us[b * tt:(b + 1) * tt, cb * LANES:(cb + 1) * LANES]
    o = fw + sw
    sa_ref[...] = jax.nn.sigmoid(_dot(h, w_ref[:, o:o + d])).reshape(NB, tt, d).astype(BF16)
    sb_ref[...] = jax.nn.sigmoid(_dot(h, w_ref[:, o + d:o + 2 * d])).reshape(NB, tt, d).astype(BF16)


def _in_proj(xv, gain, w_in, fw, sw, tt):
    nb, sseg, d = xv.shape
    n = sseg // tt
    kern = functools.partial(_in_proj_kernel, tt=tt, fw=fw, sw=sw, d=d)
    return pl.pallas_call(
        kern,
        grid=(n,),
        in_specs=[
            pl.BlockSpec((NB, tt, d), lambda i: (0, i, 0)),
            _const_spec((1, d)),
            _const_spec(w_in.shape),
        ],
        out_specs=[
            pl.BlockSpec((fw // LANES, NB, tt, LANES), lambda i: (0, 0, i, 0)),
            pl.BlockSpec((sw // LANES, NB * tt, LANES), lambda i: (0, i, 0)),
            pl.BlockSpec((NB, tt, d), lambda i: (0, i, 0)),
            pl.BlockSpec((NB, tt, d), lambda i: (0, i, 0)),
        ],
        out_shape=[
            jax.ShapeDtypeStruct((fw // LANES, NB, sseg, LANES), F32),
            jax.ShapeDtypeStruct((sw // LANES, sseg * NB, LANES), F32),
            jax.ShapeDtypeStruct((NB, sseg, d), BF16),
            jax.ShapeDtypeStruct((NB, sseg, d), BF16),
        ],
        compiler_params=pltpu.CompilerParams(
            dimension_semantics=("arbitrary",), vmem_limit_bytes=VMEM_LIMIT),
        name="in_proj",
    )(xv, gain, w_in)


def _fft_tables(s):
    n2 = FFT_N2
    n1 = s // n2
    s1 = np.arange(n1, dtype=np.int64)
    k1 = np.arange(n1, dtype=np.int64)
    s2 = np.arange(n2, dtype=np.int64)
    idx = (k1[None, :, None] * s1[None, None, :] * n2 + s2[:, None, None] * k1[None, :, None]) % s
    ang = 2.0 * np.pi * idx.astype(np.float64) / s
    g = np.stack([np.cos(ang), -np.sin(ang)], axis=2) / math.sqrt(n1)
    g = g.reshape(n2, 2 * n1, n1)
    k2 = np.arange(n2, dtype=np.int64)
    ang2 = 2.0 * np.pi * ((k2[:, None] * s2[None, :]) % n2).astype(np.float64) / n2
    c2, s2m = np.cos(ang2), np.sin(ang2)
    h = np.block([[c2, s2m], [-s2m, c2]]) / math.sqrt(n2)
    c = np.arange(FNET_GROUP_DIM, dtype=np.int64)
    angc = 2.0 * np.pi * ((c[:, None] * c[None, :]) % FNET_GROUP_DIM).astype(np.float64) / FNET_GROUP_DIM
    cs = np.concatenate([np.cos(angc), np.sin(angc)], axis=0) / math.sqrt(FNET_GROUP_DIM)
    return g.astype(np.float32), h.astype(np.float32), cs.astype(np.float32)


def _fft1_kernel(x_ref, g_ref, o_ref, *, n1, ng):
    for j in range(SUBLANES):
        xj = jnp.concatenate([x_ref[g, :, j, :] for g in range(ng)], axis=1).astype(BF16)
        r = _dot(g_ref[j], xj)
        for g in range(ng):
            o_ref[g, :, j, :] = r[:, g * LANES:(g + 1) * LANES]


def _fft2_kernel(a_ref, h_ref, cs_ref, o_ref, *, k1blk, ng, n2):
    for j in range(k1blk):
        a = jnp.concatenate([a_ref[g, j] for g in range(ng)], axis=1).astype(BF16)
        y = _dot(h_ref[...], a).astype(BF16)
        for g in range(ng):
            yg = jnp.concatenate(
                [y[0:n2, g * LANES:(g + 1) * LANES], y[n2:2 * n2, g * LANES:(g + 1) * LANES]], axis=1)
            o_ref[g, :, j, :] = _dot(yg, cs_ref[...])


def _fourier_mix(u, k1blk=SUBLANES):
    ng, b, s, c = u.shape
    assert c == FNET_GROUP_DIM == LANES
    n2 = FFT_N2
    n1 = s // n2
    assert n1 * n2 == s and n1 % k1blk == 0
    g_np, h_np, cs_np = _fft_tables(s)
    g_t = jnp.asarray(g_np).astype(BF16)
    h_t = jnp.asarray(h_np).astype(BF16)
    cs_t = jnp.asarray(cs_np).astype(BF16)

    a = pl.pallas_call(
        functools.partial(_fft1_kernel, n1=n1, ng=ng),
        grid=(b, n2 // SUBLANES),
        in_specs=[
            pl.BlockSpec((ng, None, n1, SUBLANES, c), lambda bi, i: (0, bi, 0, i, 0)),
            pl.BlockSpec((SUBLANES, 2 * n1, n1), lambda bi, i: (i, 0, 0)),
        ],
        out_specs=pl.BlockSpec((ng, None, 2 * n1, SUBLANES, c), lambda bi, i: (0, bi, 0, i, 0)),
        out_shape=jax.ShapeDtypeStruct((ng, b, 2 * n1, n2, c), F32),
        compiler_params=pltpu.CompilerParams(
            dimension_semantics=("arbitrary", "arbitrary"), vmem_limit_bytes=VMEM_LIMIT),
        name="fft_stage1",
    )(u.reshape(ng, b, n1, n2, c), g_t)

    f = pl.pallas_call(
        functools.partial(_fft2_kernel, k1blk=k1blk, ng=ng, n2=n2),
        grid=(b, n1 // k1blk),
        in_specs=[
            pl.BlockSpec((ng, None, k1blk, 2 * n2, c), lambda bi, i: (0, bi, i, 0, 0)),
            _const_spec((2 * n2, 2 * n2)),
            _const_spec((2 * FNET_GROUP_DIM, FNET_GROUP_DIM)),
        ],
        out_specs=pl.BlockSpec((ng, None, n2, k1blk, c), lambda bi, i: (0, bi, 0, i, 0)),
        out_shape=jax.ShapeDtypeStruct((ng, b, n2, n1, c), F32),
        compiler_params=pltpu.CompilerParams(
            dimension_semantics=("arbitrary", "arbitrary"), vmem_limit_bytes=VMEM_LIMIT),
        name="fft_stage2",
    )(a.reshape(ng, b, n1, 2 * n2, c), h_t, cs_t)
    return f.reshape(ng, b, s, c)


def _ssm_tables(lam_re, lam_im, log_dt, b_re, b_im, c_re, c_im, seg_len):
    ndir, ngroups, p = lam_re.shape
    hdim = b_re.shape[-1]
    nblk = ngroups // GROUPS_PER_BLOCK
    lam_re = lam_re.astype(F32)
    lam_im = lam_im.astype(F32)
    dt = jnp.exp(log_dt.astype(F32))[..., None]
    mag = jnp.exp(lam_re * dt)
    ar = mag * jnp.cos(lam_im * dt)
    ai = mag * jnp.sin(lam_im * dt)
    den = lam_re * lam_re + lam_im * lam_im
    qr = ((ar - 1.0) * lam_re + ai * lam_im) / den
    qi = (ai * lam_re - (ar - 1.0) * lam_im) / den
    bre = qr[..., None] * b_re.astype(F32) - qi[..., None] * b_im.astype(F32)
    bim = qr[..., None] * b_im.astype(F32) + qi[..., None] * b_re.astype(F32)
    magl = jnp.exp(lam_re * dt * seg_len)
    alr = magl * jnp.cos(lam_im * dt * seg_len)
    ali = magl * jnp.sin(lam_im * dt * seg_len)

    eye = jnp.eye(GROUPS_PER_BLOCK, dtype=F32)

    def blockdiag_in(w):
        w = w.reshape(ndir, nblk, GROUPS_PER_BLOCK, p, hdim)
        m = jnp.einsum('dkgph,gj->dkghjp', w, eye)
        return m.reshape(ndir, nblk, GROUPS_PER_BLOCK * hdim, GROUPS_PER_BLOCK * p)

    def blockdiag_out(w):
        w = w.reshape(ndir, nblk, GROUPS_PER_BLOCK, hdim, p)
        m = jnp.einsum('dkghp,gj->dkgpjh', w, eye)
        return m.reshape(ndir, nblk, GROUPS_PER_BLOCK * p, GROUPS_PER_BLOCK * hdim)

    bcat = jnp.concatenate([blockdiag_in(bre), blockdiag_in(bim)], axis=-1).astype(BF16)
    ccat = jnp.concatenate([blockdiag_out(c_re.astype(F32)),
                            -blockdiag_out(c_im.astype(F32))], axis=-2).astype(BF16)

    def lanes(v):
        v = v.reshape(ndir, nblk, 1, STATES_PER_BLOCK)
        return jnp.broadcast_to(v, (ndir, nblk, SUBLANES, STATES_PER_BLOCK))

    return bcat, ccat, lanes(ar), lanes(ai), lanes(alr), lanes(ali)


def _ssm_kernel(*refs, t_chunk, nblk, chain, need_y):
    if need_y:
        (uf_ref, ub_ref, e_ref, bcat_ref, ccat_ref, are_ref, aim_ref, alre_ref, alim_ref, dsk_ref,
         yf_ref, yb_ref, eo_ref, s_ref, st_ref) = refs
    else:
        (uf_ref, ub_ref, e_ref, bcat_ref, ccat_ref, are_ref, aim_ref, alre_ref, alim_ref, dsk_ref,
         eo_ref, s_ref, st_ref) = refs
        yf_ref = yb_ref = None
    i = pl.program_id(0)
    n = pl.num_programs(0)
    sp = STATES_PER_BLOCK

    @pl.when(i == 0)
    def _init():
        if not chain:
            st_ref[...] = jnp.zeros(st_ref.shape, F32)
        else:
            row = lax.broadcasted_iota(jnp.int32, (SUBLANES, sp), 0)
            for d in range(2):
                shift = 1 if d == 0 else SUBLANES - 1
                keep = (row >= 1) if d == 0 else (row <= SUBLANES - 2)
                for cb in range(nblk):
                    er = pltpu.roll(e_ref[d, cb, :, 0:sp], shift, 0)
                    ei = pltpu.roll(e_ref[d, cb, :, sp:2 * sp], shift, 0)
                    lr = alre_ref[d, cb]
                    li = alim_ref[d, cb]
                    xr = jnp.zeros((SUBLANES, sp), F32)
                    xi = jnp.zeros((SUBLANES, sp), F32)
                    for _ in range(SUBLANES - 1):
                        pr = pltpu.roll(xr, shift, 0)
                        pi = pltpu.roll(xi, shift, 0)
                        xr = jnp.where(keep, lr * pr - li * pi + er, 0.0)
                        xi = jnp.where(keep, lr * pi + li * pr + ei, 0.0)
                    st_ref[d, cb, :, 0:sp] = xr
                    st_ref[d, cb, :, sp:2 * sp] = xi

    for d in range(2):
        u_ref = uf_ref if d == 0 else ub_ref
        y_ref = yf_ref if d == 0 else yb_ref
        for cb in range(nblk):
            u = u_ref[cb]
            s_ref[...] = _dot(u.astype(BF16), bcat_ref[d, cb])
            a_re = are_ref[d, cb]
            a_im = aim_ref[d, cb]

            def body(k, carry, d=d, a_re=a_re, a_im=a_im):
                re, im = carry
                t = k if d == 0 else t_chunk - 1 - k
                r = pl.multiple_of(t * SUBLANES, SUBLANES)
                n_re = a_re * re - a_im * im + s_ref[pl.ds(r, SUBLANES), 0:sp]
                n_im = a_re * im + a_im * re + s_ref[pl.ds(r, SUBLANES), sp:2 * sp]
                s_ref[pl.ds(r, SUBLANES), 0:sp] = n_re
                s_ref[pl.ds(r, SUBLANES), sp:2 * sp] = n_im
                return n_re, n_im

            re, im = lax.fori_loop(
                0, t_chunk, body, (st_ref[d, cb, :, 0:sp], st_ref[d, cb, :, sp:2 * sp]), unroll=4)
            st_ref[d, cb, :, 0:sp] = re
            st_ref[d, cb, :, sp:2 * sp] = im
            if need_y:
                y = _dot(s_ref[...].astype(BF16), ccat_ref[d, cb])
                if d == 0:
                    y = y + dsk_ref[:, cb * LANES:(cb + 1) * LANES] * u
                y_ref[cb] = y

    @pl.when(i == n - 1)
    def _fin():
        eo_ref[...] = st_ref[...]


def _ssm(us, e_in, tables, d_skip, t_chunk, chain, need_y):
    bcat, ccat, are, aim, alre, alim = tables
    nblk, rows_total, _ = us.shape
    w = nblk * LANES
    sseg = rows_total // NB
    n = sseg // t_chunk
    rows = t_chunk * NB
    st_shape = (2, nblk, SUBLANES, 2 * STATES_PER_BLOCK)
    kern = functools.partial(_ssm_kernel, t_chunk=t_chunk, nblk=nblk, chain=chain, need_y=need_y)
    out_specs = [_const_spec(st_shape)]
    out_shape = [jax.ShapeDtypeStruct(st_shape, F32)]
    if need_y:
        out_specs = [pl.BlockSpec((nblk, rows, LANES), lambda i: (0, i, 0)),
                     pl.BlockSpec((nblk, rows, LANES), lambda i: (0, n - 1 - i, 0))] + out_specs
        out_shape = [jax.ShapeDtypeStruct((nblk, rows_total, LANES), F32),
                     jax.ShapeDtypeStruct((nblk, rows_total, LANES), F32)] + out_shape
    return pl.pallas_call(
        kern,
        grid=(n,),
        in_specs=[
            pl.BlockSpec((nblk, rows, LANES), lambda i: (0, i, 0)),
            pl.BlockSpec((nblk, rows, LANES), lambda i: (0, n - 1 - i, 0)),
            _const_spec(st_shape),
            _const_spec(bcat.shape),
            _const_spec(ccat.shape),
            _const_spec(are.shape),
            _const_spec(aim.shape),
            _const_spec(alre.shape),
            _const_spec(alim.shape),
            _const_spec((1, w)),
        ],
        out_specs=out_specs,
        out_shape=out_shape,
        scratch_shapes=[
            pltpu.VMEM((rows, 2 * STATES_PER_BLOCK), F32),
            pltpu.VMEM(st_shape, F32),
        ],
        compiler_params=pltpu.CompilerParams(
            dimension_semantics=("arbitrary",), vmem_limit_bytes=VMEM_LIMIT),
        name="ssm_scan_y" if need_y else "ssm_scan_states",
    )(us, us, e_in, bcat, ccat, are, aim, alre, alim, d_skip)


def _ffn_chunks(hidden, step=1024):
    edges = list(range(0, hidden, step)) + [hidden]
    return list(zip(edges[:-1], edges[1:]))


def _mix_ffn_kernel(x_ref, f_ref, yf_ref, yb_ref, sa_ref, sb_ref, gpost_ref, gpre_ref, gfpost_ref,
                    wfo_ref, wval_ref, wgate_ref, wout_ref, wfg_ref, wfu_ref, wfd_ref, o_ref,
                    *, tt, d, fw, sw, hidden):
    rows = NB * tt
    ys = jnp.concatenate(
        [jnp.concatenate(
            [yf_ref[cb, pl.ds(b, tt, stride=NB), :] + yb_ref[cb, pl.ds(b, tt, stride=NB), :]
             for b in range(NB)], axis=0)
         for cb in range(sw // LANES)], axis=1)
    z = jax.nn.gelu(ys).astype(BF16)
    f = jnp.concatenate(
        [f_ref[g].reshape(rows, LANES) for g in range(fw // LANES)], axis=1).astype(BF16)
    br_a = _dot(f, wfo_ref[...])
    br_b = _dot(z, wval_ref[...]) * jax.nn.sigmoid(_dot(z, wgate_ref[...]))
    sa = sa_ref[...].reshape(rows, d).astype(F32)
    sb = sb_ref[...].reshape(rows, d).astype(F32)
    merged = (sa * br_a + sb * br_b).astype(BF16)
    m = _dot(merged, wout_ref[...])
    x1 = x_ref[...].reshape(rows, d) + _rms(m, gpost_ref[...])
    h2 = _rms(x1, gpre_ref[...]).astype(BF16)
    acc = None
    for c0, c1 in _ffn_chunks(hidden):
        g = _dot(h2, wfg_ref[:, c0:c1])
        u = _dot(h2, wfu_ref[:, c0:c1])
        part = _dot((jax.nn.silu(g) * u).astype(BF16), wfd_ref[c0:c1, :])
        acc = part if acc is None else acc + part
    o_ref[...] = (x1 + _rms(acc, gfpost_ref[...])).reshape(NB, tt, d)


def _mix_ffn(xv, f, yf, yb, sa, sb, gpost, gpre, gfpost, wfo, wval, wgate, wout, wfg, wfu, wfd, tt):
    nb, sseg, d = xv.shape
    fw = f.shape[0] * LANES
    nblk = yf.shape[0]
    sw = nblk * LANES
    hidden = wfg.shape[-1]
    n = sseg // tt
    kern = functools.partial(_mix_ffn_kernel, tt=tt, d=d, fw=fw, sw=sw, hidden=hidden)
    tok = lambda width: pl.BlockSpec((NB, tt, width), lambda i: (0, i, 0))
    tmaj = pl.BlockSpec((nblk, NB * tt, LANES), lambda i: (0, i, 0))
    return pl.pallas_call(
        kern,
        grid=(n,),
        in_specs=[tok(d), pl.BlockSpec((fw // LANES, NB, tt, LANES), lambda i: (0, 0, i, 0)),
                  tmaj, tmaj, tok(d), tok(d),
                  _const_spec((1, d)), _const_spec((1, d)), _const_spec((1, d)),
                  _const_spec(wfo.shape), _const_spec(wval.shape), _const_spec(wgate.shape),
                  _const_spec(wout.shape), _const_spec(wfg.shape), _const_spec(wfu.shape),
                  _const_spec(wfd.shape)],
        out_specs=tok(d),
        out_shape=jax.ShapeDtypeStruct((NB, sseg, d), F32),
        compiler_params=pltpu.CompilerParams(
            dimension_semantics=("arbitrary",), vmem_limit_bytes=VMEM_LIMIT),
        name="mix_ffn",
    )(xv, f, yf, yb, sa, sb, gpost, gpre, gfpost, wfo, wval, wgate, wout, wfg, wfu, wfd)


def _pick(total, want):
    t = min(want, total)
    assert total % t == 0, (total, t)
    return t


def _encoder_layer(x, p):
    b, s, d = x.shape
    assert b in (1, NB), "one sequence (split in NB segments) or NB sequences"
    sseg = (b * s) // NB
    fw = p["w_fnet_out"].shape[0]
    sw = p["w_glu_val"].shape[0]
    xv = x.reshape(NB, sseg, d)
    tt = _pick(sseg, 64)

    uf, us, sa, sb = _in_proj(xv, p["norm_mix_pre"], p["w_in"], fw, sw, tt)
    ng = fw // LANES
    f = _fourier_mix(uf.reshape(ng, b, s, LANES)).reshape(ng, NB, sseg, LANES)

    tables = _ssm_tables(p["lam_re"], p["lam_im"], p["log_dt"], p["b_re"], p["b_im"],
                         p["c_re"], p["c_im"], sseg)
    t_chunk = _pick(sseg, 128)
    nblk = sw // LANES
    e0 = jnp.zeros((2, nblk, SUBLANES, 2 * STATES_PER_BLOCK), F32)
    if b == NB:
        yf, yb, _ = _ssm(us, e0, tables, p["d_skip"], t_chunk, chain=False, need_y=True)
    else:
        (e,) = _ssm(us, e0, tables, p["d_skip"], t_chunk, chain=False, need_y=False)
        yf, yb, _ = _ssm(us, e, tables, p["d_skip"], t_chunk, chain=True, need_y=True)

    y = _mix_ffn(xv, f, yf, yb, sa, sb, p["norm_mix_post"], p["norm_ffn_pre"], p["norm_ffn_post"],
                 p["w_fnet_out"], p["w_glu_val"], p["w_glu_gate"], p["w_out"],
                 p["w_ffn_gate"], p["w_ffn_up"], p["w_ffn_down"], tt)
    return y.reshape(b, s, d)


def _layer_params(l, norm_mix_pre, norm_mix_post, norm_ffn_pre, norm_ffn_post, w_in, w_fnet_out,
                  lam_re, lam_im, log_dt, b_re, b_im, c_re, c_im, d_skip, w_glu_val, w_glu_gate,
                  w_out, w_ffn_gate, w_ffn_up, w_ffn_down):
    row = lambda v: v[l].astype(F32).reshape(1, -1)
    w = lambda v: v[l].astype(BF16)
    return dict(
        norm_mix_pre=row(norm_mix_pre), norm_mix_post=row(norm_mix_post),
        norm_ffn_pre=row(norm_ffn_pre), norm_ffn_post=row(norm_ffn_post),
        w_in=w(w_in), w_fnet_out=w(w_fnet_out),
        lam_re=lam_re[l], lam_im=lam_im[l], log_dt=log_dt[l],
        b_re=b_re[l], b_im=b_im[l], c_re=c_re[l], c_im=c_im[l], d_skip=row(d_skip),
        w_glu_val=w(w_glu_val), w_glu_gate=w(w_glu_gate), w_out=w(w_out),
        w_ffn_gate=w(w_ffn_gate), w_ffn_up=w(w_ffn_up), w_ffn_down=w(w_ffn_down))


def kernel(x_prompt, x_sample, norm_mix_pre, norm_mix_post, norm_ffn_pre, norm_ffn_post, w_in, w_fnet_out, lam_re, lam_im, log_dt, b_re, b_im, c_re, c_im, d_skip, w_glu_val, w_glu_gate, w_out, w_ffn_gate, w_ffn_up, w_ffn_down):
    weights = (norm_mix_pre, norm_mix_post, norm_ffn_pre, norm_ffn_post, w_in, w_fnet_out,
               lam_re, lam_im, log_dt, b_re, b_im, c_re, c_im, d_skip, w_glu_val, w_glu_gate,
               w_out, w_ffn_gate, w_ffn_up, w_ffn_down)
    depth = w_in.shape[0]
    outs = []
    for x in (x_prompt, x_sample):
        for l in range(depth):
            x = _encoder_layer(x, _layer_params(l, *weights))
        outs.append(x)
    return tuple(outs)
```

```python
import functools
import math

import numpy as np
import jax
import jax.numpy as jnp
from jax import lax
from jax.experimental import pallas as pl
from jax.experimental.pallas import tpu as pltpu

F32 = jnp.float32
BF16 = jnp.bfloat16

EPS = 1e-6
SUBLANES = 8
LANES = 128
NB = SUBLANES

FNET_GROUP_DIM = 128
SSM_GROUP_DIM = 16
GROUPS_PER_BLOCK = LANES // SSM_GROUP_DIM
SCAN_BLOCK = LANES // SSM_GROUP_DIM
PAIR = 2
SCAN_INTERLEAVE = 4

FFT_N2 = 128
VMEM_LIMIT = 56 * 1024 * 1024


def _dot(a, b):
    return jnp.dot(a, b, preferred_element_type=F32)


def _rms(x, g):
    return x * lax.rsqrt(jnp.mean(x * x, axis=-1, keepdims=True) + EPS) * g


def _const_spec(shape):
    nd = len(shape)
    return pl.BlockSpec(shape, lambda *_: (0,) * nd, pipeline_mode=pl.Buffered(1))


def _in_proj_kernel(x_ref, g_ref, w_ref, uf_ref, us_ref, sa_ref, sb_ref, *, tt, fw, sw, d):
    rows = NB * tt
    x = x_ref[...].reshape(rows, d)
    h = _rms(x, g_ref[...]).astype(BF16)
    uf = _dot(h, w_ref[:, 0:fw])
    for g in range(fw // LANES):
        uf_ref[g] = uf[:, g * LANES:(g + 1) * LANES].reshape(NB, tt, LANES)
    us = _dot(h, w_ref[:, fw:fw + sw])
    for cb in range(sw // LANES):
        for b in range(NB):
            us_ref[cb, pl.ds(b, tt, stride=NB), :] = us[b * tt:(b + 1) * tt, cb * LANES:(cb + 1) * LANES]
    o = fw + sw
    sa_ref[...] = jax.nn.sigmoid(_dot(h, w_ref[:, o:o + d])).reshape(NB, tt, d).astype(BF16)
    sb_ref[...] = jax.nn.sigmoid(_dot(h, w_ref[:, o + d:o + 2 * d])).reshape(NB, tt, d).astype(BF16)


def _in_proj(xv, gain, w_in, fw, sw, tt):
    nb, sseg, d = xv.shape
    n = sseg // tt
    kern = functools.partial(_in_proj_kernel, tt=tt, fw=fw, sw=sw, d=d)
    return pl.pallas_call(
        kern,
        grid=(n,),
        in_specs=[
            pl.BlockSpec((NB, tt, d), lambda i: (0, i, 0)),
            _const_spec((1, d)),
            _const_spec(w_in.shape),
        ],
        out_specs=[
            pl.BlockSpec((fw // LANES, NB, tt, LANES), lambda i: (0, 0, i, 0)),
            pl.BlockSpec((sw // LANES, NB * tt, LANES), lambda i: (0, i, 0)),
            pl.BlockSpec((NB, tt, d), lambda i: (0, i, 0)),
            pl.BlockSpec((NB, tt, d), lambda i: (0, i, 0)),
        ],
        out_shape=[
            jax.ShapeDtypeStruct((fw // LANES, NB, sseg, LANES), F32),
            jax.ShapeDtypeStruct((sw // LANES, sseg * NB, LANES), F32),
            jax.ShapeDtypeStruct((NB, sseg, d), BF16),
            jax.ShapeDtypeStruct((NB, sseg, d), BF16),
        ],
        compiler_params=pltpu.CompilerParams(
            dimension_semantics=("arbitrary",), vmem_limit_bytes=VMEM_LIMIT),
        name="in_proj",
    )(xv, gain, w_in)


def _fft_tables(s):
    n2 = FFT_N2
    n1 = s // n2
    s1 = np.arange(n1, dtype=np.int64)
    k1 = np.arange(n1, dtype=np.int64)
    s2 = np.arange(n2, dtype=np.int64)
    idx = (k1[None, :, None] * s1[None, None, :] * n2 + s2[:, None, None] * k1[None, :, None]) % s
    ang = 2.0 * np.pi * idx.astype(np.float64) / s
    g = np.stack([np.cos(ang), -np.sin(ang)], axis=2) / math.sqrt(n1)
    g = g.reshape(n2, 2 * n1, n1)
    k2 = np.arange(n2, dtype=np.int64)
    ang2 = 2.0 * np.pi * ((k2[:, None] * s2[None, :]) % n2).astype(np.float64) / n2
    c2, s2m = np.cos(ang2), np.sin(ang2)
    h = np.block([[c2, s2m], [-s2m, c2]]) / math.sqrt(n2)
    c = np.arange(FNET_GROUP_DIM, dtype=np.int64)
    angc = 2.0 * np.pi * ((c[:, None] * c[None, :]) % FNET_GROUP_DIM).astype(np.float64) / FNET_GROUP_DIM
    cs = np.concatenate([np.cos(angc), np.sin(angc)], axis=0) / math.sqrt(FNET_GROUP_DIM)
    return g.astype(np.float32), h.astype(np.float32), cs.astype(np.float32)


def _fft1_kernel(x_ref, g_ref, o_ref, *, n1, ng):
    for j in range(SUBLANES):
        xj = jnp.concatenate([x_ref[g, :, j, :] for g in range(ng)], axis=1).astype(BF16)
        r = _dot(g_ref[j], xj)
        for g in range(ng):
            o_ref[g, :, j, :] = r[:, g * LANES:(g + 1) * LANES]


def _fft2_kernel(a_ref, h_ref, cs_ref, o_ref, *, k1blk, ng, n2):
    for j in range(k1blk):
        a = jnp.concatenate([a_ref[g, j] for g in range(ng)], axis=1).astype(BF16)
        y = _dot(h_ref[...], a).astype(BF16)
        for g in range(ng):
            yg = jnp.concatenate(
                [y[0:n2, g * LANES:(g + 1) * LANES], y[n2:2 * n2, g * LANES:(g + 1) * LANES]], axis=1)
            o_ref[g, :, j, :] = _dot(yg, cs_ref[...])


def _fourier_mix(u, k1blk=SUBLANES):
    ng, b, s, c = u.shape
    assert c == FNET_GROUP_DIM == LANES
    n2 = FFT_N2
    n1 = s // n2
    assert n1 * n2 == s and n1 % k1blk == 0
    g_np, h_np, cs_np = _fft_tables(s)
    g_t = jnp.asarray(g_np).astype(BF16)
    h_t = jnp.asarray(h_np).astype(BF16)
    cs_t = jnp.asarray(cs_np).astype(BF16)

    a = pl.pallas_call(
        functools.partial(_fft1_kernel, n1=n1, ng=ng),
        grid=(b, n2 // SUBLANES),
        in_specs=[
            pl.BlockSpec((ng, None, n1, SUBLANES, c), lambda bi, i: (0, bi, 0, i, 0)),
            pl.BlockSpec((SUBLANES, 2 * n1, n1), lambda bi, i: (i, 0, 0)),
        ],
        out_specs=pl.BlockSpec((ng, None, 2 * n1, SUBLANES, c), lambda bi, i: (0, bi, 0, i, 0)),
        out_shape=jax.ShapeDtypeStruct((ng, b, 2 * n1, n2, c), F32),
        compiler_params=pltpu.CompilerParams(
            dimension_semantics=("arbitrary", "arbitrary"), vmem_limit_bytes=VMEM_LIMIT),
        name="fft_stage1",
    )(u.reshape(ng, b, n1, n2, c), g_t)

    f = pl.pallas_call(
        functools.partial(_fft2_kernel, k1blk=k1blk, ng=ng, n2=n2),
        grid=(b, n1 // k1blk),
        in_specs=[
            pl.BlockSpec((ng, None, k1blk, 2 * n2, c), lambda bi, i: (0, bi, i, 0, 0)),
            _const_spec((2 * n2, 2 * n2)),
            _const_spec((2 * FNET_GROUP_DIM, FNET_GROUP_DIM)),
        ],
        out_specs=pl.BlockSpec((ng, None, n2, k1blk, c), lambda bi, i: (0, bi, 0, i, 0)),
        out_shape=jax.ShapeDtypeStruct((ng, b, n2, n1, c), F32),
        compiler_params=pltpu.CompilerParams(
            dimension_semantics=("arbitrary", "arbitrary"), vmem_limit_bytes=VMEM_LIMIT),
        name="fft_stage2",
    )(a.reshape(ng, b, n1, 2 * n2, c), h_t, cs_t)
    return f.reshape(ng, b, s, c)


def _ssm_tables(lam_re, lam_im, log_dt, b_re, b_im, c_re, c_im, d_skip, seg_len):
    hp = lax.Precision.HIGHEST
    ndir, ng, p = lam_re.shape
    hd = b_re.shape[-1]
    r = SCAN_BLOCK
    nq = ng // PAIR
    lam_re = lam_re.astype(F32)
    lam_im = lam_im.astype(F32)
    dt = jnp.exp(log_dt.astype(F32))[..., None]

    def apow(k):
        mag = jnp.exp(lam_re * dt * k)
        return mag * jnp.cos(lam_im * dt * k), mag * jnp.sin(lam_im * dt * k)

    ar, ai = apow(1.0)
    den = lam_re * lam_re + lam_im * lam_im
    qr = ((ar - 1.0) * lam_re + ai * lam_im) / den
    qi = (ai * lam_re - (ar - 1.0) * lam_im) / den
    bre = qr[..., None] * b_re.astype(F32) - qi[..., None] * b_im.astype(F32)
    bim = qr[..., None] * b_im.astype(F32) + qi[..., None] * b_re.astype(F32)
    cr = c_re.astype(F32)
    ci = c_im.astype(F32)

    ks = jnp.arange(r + 1, dtype=F32).reshape(r + 1, 1, 1, 1)
    pr, pi = apow(ks)
    abr = pr[:r, ..., None] * bre - pi[:r, ..., None] * bim
    abi = pr[:r, ..., None] * bim + pi[:r, ..., None] * bre

    def in_rows(x):
        return jnp.transpose(x, (1, 0, 3, 2)).reshape(ng, r * hd, p)

    bp_re = jnp.stack([in_rows(abr[::-1, 0]), in_rows(abr[:, 1])])
    bp_im = jnp.stack([in_rows(abi[::-1, 0]), in_rows(abi[:, 1])])

    klag = (jnp.einsum('dgop,ldgph->ldgoh', cr, abr, precision=hp)
            - jnp.einsum('dgop,ldgph->ldgoh', ci, abi, precision=hp))
    jj = jnp.arange(r)[:, None]
    ii = jnp.arange(r)[None, :]
    kf = klag[:, 0][jnp.maximum(ii - jj, 0)] * (ii >= jj)[..., None, None, None]
    kb = klag[:, 1][jnp.maximum(jj - ii, 0)] * (jj >= ii)[..., None, None, None]
    kin = jnp.transpose(kf + kb, (2, 0, 4, 1, 3)).reshape(ng, r * hd, r * hd)
    dsk = d_skip.astype(F32).reshape(ng, hd)
    kin = kin + jnp.eye(r * hd, dtype=F32)[None] * jnp.tile(dsk, (1, r))[:, None, :]

    def state_out(d, pows):
        er = cr[d][None] * pr[pows, d][:, :, None, :] - ci[d][None] * pi[pows, d][:, :, None, :]
        ei = cr[d][None] * pi[pows, d][:, :, None, :] + ci[d][None] * pr[pows, d][:, :, None, :]
        to_rows = lambda e: jnp.transpose(e, (1, 3, 0, 2)).reshape(ng, p, r * hd)
        return to_rows(er), -to_rows(ei)

    cf_re, cf_im = state_out(0, jnp.arange(1, r + 1))
    cb_re, cb_im = state_out(1, r - jnp.arange(r))

    eye2 = jnp.eye(PAIR, dtype=F32)

    def pair_bd(m):
        lead = m.shape[:-3]
        rr, cc = m.shape[-2:]
        m = m.reshape(lead + (nq, PAIR, rr, cc))
        out = jnp.einsum('...qarc,ab->...qarbc', m, eye2, precision=hp)
        return out.reshape(lead + (nq, PAIR * rr, PAIR * cc))

    bq = jnp.concatenate([pair_bd(bp_re), pair_bd(bp_im)], axis=-1).astype(BF16)
    wy = jnp.concatenate([pair_bd(kin), pair_bd(cf_re), pair_bd(cf_im),
                          pair_bd(cb_re), pair_bd(cb_im)], axis=-2).astype(BF16)

    def lanes(v):
        v = v.reshape(ndir, nq, 1, PAIR * p)
        return jnp.broadcast_to(v, (ndir, nq, SUBLANES, PAIR * p))

    a8r, a8i = apow(float(r))
    alr, ali = apow(float(seg_len))
    return bq, wy, lanes(a8r), lanes(a8i), lanes(alr), lanes(ali)


def _fold_lanes(vs, k):
    rows = vs[0].shape[0]
    piece = lax.broadcasted_iota(jnp.int32, (rows, LANES), 1) // SSM_GROUP_DIM
    acc = None
    for c in range(GROUPS_PER_BLOCK):
        s = ((c - k) % GROUPS_PER_BLOCK) * SSM_GROUP_DIM
        v = vs[c] if s == 0 else pltpu.roll(vs[c], s, 1)
        acc = v if acc is None else jnp.where(piece == c, v, acc)
    return acc


def _scan_kernel(*refs, mc, nq, ncb, reverse, relayout, chain):
    if relayout:
        (src_ref, e_ref, bq_ref, a8re_ref, a8im_ref, alre_ref, alim_ref,
         u8_ref, x_ref, eo_ref, d_scr, st_scr) = refs
    else:
        (u8_ref, e_ref, bq_ref, a8re_ref, a8im_ref, alre_ref, alim_ref,
         x_ref, eo_ref, d_scr, st_scr) = refs
    i = pl.program_id(0)
    n = pl.num_programs(0)
    rows = mc * SUBLANES
    hl = LANES

    @pl.when(i == 0)
    def _init():
        if not chain:
            st_scr[...] = jnp.zeros(st_scr.shape, F32)
        else:
            row = lax.broadcasted_iota(jnp.int32, (SUBLANES, hl), 0)
            shift = SUBLANES - 1 if reverse else 1
            keep = (row <= SUBLANES - 2) if reverse else (row >= 1)
            for q in range(nq):
                er = pltpu.roll(e_ref[q, :, 0:hl], shift, 0)
                ei = pltpu.roll(e_ref[q, :, hl:2 * hl], shift, 0)
                lr = alre_ref[q]
                li = alim_ref[q]
                xr = jnp.zeros((SUBLANES, hl), F32)
                xi = jnp.zeros((SUBLANES, hl), F32)
                for _ in range(SUBLANES - 1):
                    pr = pltpu.roll(xr, shift, 0)
                    pi = pltpu.roll(xi, shift, 0)
                    xr = jnp.where(keep, lr * pr - li * pi + er, 0.0)
                    xi = jnp.where(keep, lr * pi + li * pr + ei, 0.0)
                st_scr[q, :, 0:hl] = xr
                st_scr[q, :, hl:2 * hl] = xi

    if relayout:
        for cb in range(ncb):
            vs = [src_ref[cb, :, j, :, :].reshape(rows, LANES) for j in range(SCAN_BLOCK)]
            for gl in range(GROUPS_PER_BLOCK):
                g = cb * GROUPS_PER_BLOCK + gl
                folded = _fold_lanes(vs, gl)
                u8_ref[g // PAIR, :, (g % PAIR) * LANES:(g % PAIR + 1) * LANES] = folded.astype(BF16)

    for q0 in range(0, nq, SCAN_INTERLEAVE):
        qs = list(range(q0, min(q0 + SCAN_INTERLEAVE, nq)))
        for k, q in enumerate(qs):
            d_scr[k] = _dot(u8_ref[q], bq_ref[q])
        a_re = [a8re_ref[q] for q in qs]
        a_im = [a8im_ref[q] for q in qs]

        def body(step, carry, a_re=a_re, a_im=a_im, nk=len(qs)):
            m = (mc - 1 - step) if reverse else step
            r = pl.multiple_of(m * SUBLANES, SUBLANES)
            out = []
            for k in range(nk):
                re, im = carry[2 * k], carry[2 * k + 1]
                d_re = d_scr[k, pl.ds(r, SUBLANES), 0:hl]
                d_im = d_scr[k, pl.ds(r, SUBLANES), hl:2 * hl]
                d_scr[k, pl.ds(r, SUBLANES), 0:hl] = re
                d_scr[k, pl.ds(r, SUBLANES), hl:2 * hl] = im
                out.append(a_re[k] * re - a_im[k] * im + d_re)
                out.append(a_re[k] * im + a_im[k] * re + d_im)
            return tuple(out)

        init = []
        for q in qs:
            init += [st_scr[q, :, 0:hl], st_scr[q, :, hl:2 * hl]]
        fin = lax.fori_loop(0, mc, body, tuple(init), unroll=2)
        for k, q in enumerate(qs):
            st_scr[q, :, 0:hl] = fin[2 * k]
            st_scr[q, :, hl:2 * hl] = fin[2 * k + 1]
            x_ref[q] = d_scr[k].astype(BF16)

    @pl.when(i == n - 1)
    def _fin():
        eo_ref[...] = st_scr[...]


def _scan(src, e_in, bq, a8re, a8im, alre, alim, mc, reverse, relayout, chain):
    nq = bq.shape[0]
    if relayout:
        ncb, mtot = src.shape[0], src.shape[1]
    else:
        ncb, mtot = 0, src.shape[1] // NB
    n = mtot // mc
    rows = mc * NB
    order = (lambda i: n - 1 - i) if reverse else (lambda i: i)
    st_shape = (nq, SUBLANES, 2 * LANES)
    row_spec = pl.BlockSpec((nq, rows, 2 * LANES), lambda i: (0, order(i), 0))
    if relayout:
        src_spec = pl.BlockSpec((ncb, mc, SCAN_BLOCK, NB, LANES), lambda i: (0, order(i), 0, 0, 0))
    else:
        src_spec = row_spec
    out_specs = [row_spec, _const_spec(st_shape)]
    out_shape = [jax.ShapeDtypeStruct((nq, mtot * NB, 2 * LANES), BF16),
                 jax.ShapeDtypeStruct(st_shape, F32)]
    if relayout:
        out_specs = [row_spec] + out_specs
        out_shape = [jax.ShapeDtypeStruct((nq, mtot * NB, 2 * LANES), BF16)] + out_shape
    kern = functools.partial(_scan_kernel, mc=mc, nq=nq, ncb=ncb, reverse=reverse,
                             relayout=relayout, chain=chain)
    return pl.pallas_call(
        kern,
        grid=(n,),
        in_specs=[src_spec, _const_spec(st_shape), _const_spec(bq.shape),
                  _const_spec(a8re.shape), _const_spec(a8im.shape),
                  _const_spec(alre.shape), _const_spec(alim.shape)],
        out_specs=out_specs,
        out_shape=out_shape,
        scratch_shapes=[pltpu.VMEM((SCAN_INTERLEAVE, rows, 2 * LANES), F32),
                        pltpu.VMEM(st_shape, F32)],
        compiler_params=pltpu.CompilerParams(
            dimension_semantics=("arbitrary",), vmem_limit_bytes=VMEM_LIMIT),
        name="ssm_scan_" + ("bwd" if reverse else "fwd") + ("_chain" if chain else ""),
    )(src, e_in, bq, a8re, a8im, alre, alim)


def _ssm_out_kernel(u8_ref, xf_ref, xb_ref, wy_ref, y_ref, ys_scr, *, mc, ncb):
    rows = mc * SUBLANES
    qpb = GROUPS_PER_BLOCK // PAIR
    for cb in range(ncb):
        for ql in range(qpb):
            q = cb * qpb + ql
            lhs = jnp.concatenate([u8_ref[q], xf_ref[q], xb_ref[q]], axis=1)
            y8 = _dot(lhs, wy_ref[q])
            for g2 in range(PAIR):
                ys_scr[PAIR * ql + g2] = y8[:, g2 * LANES:(g2 + 1) * LANES]
        vs = [ys_scr[gl] for gl in range(GROUPS_PER_BLOCK)]
        for i in range(SCAN_BLOCK):
            y_ref[cb, :, i, :, :] = _fold_lanes(vs, i).reshape(mc, NB, LANES)


def _ssm_out(u8, xf, xb, wy, ncb, mc):
    nq, rows_total, _ = u8.shape
    mtot = rows_total // NB
    n = mtot // mc
    rows = mc * NB
    row_spec = pl.BlockSpec((nq, rows, 2 * LANES), lambda i: (0, i, 0))
    return pl.pallas_call(
        functools.partial(_ssm_out_kernel, mc=mc, ncb=ncb),
        grid=(n,),
        in_specs=[row_spec, row_spec, row_spec, _const_spec(wy.shape)],
        out_specs=pl.BlockSpec((ncb, mc, SCAN_BLOCK, NB, LANES), lambda i: (0, i, 0, 0, 0)),
        out_shape=jax.ShapeDtypeStruct((ncb, mtot, SCAN_BLOCK, NB, LANES), F32),
        scratch_shapes=[pltpu.VMEM((GROUPS_PER_BLOCK, rows, LANES), F32)],
        compiler_params=pltpu.CompilerParams(
            dimension_semantics=("arbitrary",), vmem_limit_bytes=VMEM_LIMIT),
        name="ssm_out",
    )(u8, xf, xb, wy)


def _ssm(us, tables, chained, mc):
    bq, wy, a8re, a8im, alre, alim = tables
    ncb, rows_tb, _ = us.shape
    sseg = rows_tb // NB
    mtot = sseg // SCAN_BLOCK
    nq = bq.shape[1]
    e0 = jnp.zeros((nq, SUBLANES, 2 * LANES), F32)
    src = us.reshape(ncb, mtot, SCAN_BLOCK, NB, LANES)
    fwd = lambda s, e, relayout, chain: _scan(
        s, e, bq[0], a8re[0], a8im[0], alre[0], alim[0], mc, False, relayout, chain)
    bwd = lambda s, e, chain: _scan(
        s, e, bq[1], a8re[1], a8im[1], alre[1], alim[1], mc, True, False, chain)
    u8, xf, ef = fwd(src, e0, True, False)
    xb, eb = bwd(u8, e0, False)
    if chained:
        xf, _ = fwd(u8, ef, False, True)
        xb, _ = bwd(u8, eb, True)
    y = _ssm_out(u8, xf, xb, wy, ncb, mc)
    return y.reshape(ncb, rows_tb, LANES)


def _ffn_chunks(hidden, step=1024):
    edges = list(range(0, hidden, step)) + [hidden]
    return list(zip(edges[:-1], edges[1:]))


def _mix_ffn_kernel(x_ref, f_ref, y_ref, sa_ref, sb_ref, gpost_ref, gpre_ref, gfpost_ref,
                    wfo_ref, wval_ref, wgate_ref, wout_ref, wfg_ref, wfu_ref, wfd_ref, o_ref,
                    *, tt, d, fw, sw, hidden):
    rows = NB * tt
    ys = jnp.concatenate(
        [jnp.concatenate(
            [y_ref[cb, pl.ds(b, tt, stride=NB), :] for b in range(NB)], axis=0)
         for cb in range(sw // LANES)], axis=1)
    z = jax.nn.gelu(ys).astype(BF16)
    f = jnp.concatenate(
        [f_ref[g].reshape(rows, LANES) for g in range(fw // LANES)], axis=1).astype(BF16)
    br_a = _dot(f, wfo_ref[...])
    br_b = _dot(z, wval_ref[...]) * jax.nn.sigmoid(_dot(z, wgate_ref[...]))
    sa = sa_ref[...].reshape(rows, d).astype(F32)
    sb = sb_ref[...].reshape(rows, d).astype(F32)
    merged = (sa * br_a + sb * br_b).astype(BF16)
    m = _dot(merged, wout_ref[...])
    x1 = x_ref[...].reshape(rows, d) + _rms(m, gpost_ref[...])
    h2 = _rms(x1, gpre_ref[...]).astype(BF16)
    acc = None
    for c0, c1 in _ffn_chunks(hidden):
        g = _dot(h2, wfg_ref[:, c0:c1])
        u = _dot(h2, wfu_ref[:, c0:c1])
        part = _dot((jax.nn.silu(g) * u).astype(BF16), wfd_ref[c0:c1, :])
        acc = part if acc is None else acc + part
    o_ref[...] = (x1 + _rms(acc, gfpost_ref[...])).reshape(NB, tt, d)


def _mix_ffn(xv, f, y, sa, sb, gpost, gpre, gfpost, wfo, wval, wgate, wout, wfg, wfu, wfd, tt):
    nb, sseg, d = xv.shape
    fw = f.shape[0] * LANES
    nblk = y.shape[0]
    sw = nblk * LANES
    hidden = wfg.shape[-1]
    n = sseg // tt
    kern = functools.partial(_mix_ffn_kernel, tt=tt, d=d, fw=fw, sw=sw, hidden=hidden)
    tok = lambda width: pl.BlockSpec((NB, tt, width), lambda i: (0, i, 0))
    tmaj = pl.BlockSpec((nblk, NB * tt, LANES), lambda i: (0, i, 0))
    return pl.pallas_call(
        kern,
        grid=(n,),
        in_specs=[tok(d), pl.BlockSpec((fw // LANES, NB, tt, LANES), lambda i: (0, 0, i, 0)),
                  tmaj, tok(d), tok(d),
                  _const_spec((1, d)), _const_spec((1, d)), _const_spec((1, d)),
                  _const_spec(wfo.shape), _const_spec(wval.shape), _const_spec(wgate.shape),
                  _const_spec(wout.shape), _const_spec(wfg.shape), _const_spec(wfu.shape),
                  _const_spec(wfd.shape)],
        out_specs=tok(d),
        out_shape=jax.ShapeDtypeStruct((NB, sseg, d), F32),
        compiler_params=pltpu.CompilerParams(
            dimension_semantics=("arbitrary",), vmem_limit_bytes=VMEM_LIMIT),
        name="mix_ffn",
    )(xv, f, y, sa, sb, gpost, gpre, gfpost, wfo, wval, wgate, wout, wfg, wfu, wfd)


def _pick(total, want):
    t = min(want, total)
    assert total % t == 0, (total, t)
    return t


def _encoder_layer(x, p):
    b, s, d = x.shape
    assert b in (1, NB), "one sequence (split in NB segments) or NB sequences"
    sseg = (b * s) // NB
    fw = p["w_fnet_out"].shape[0]
    sw = p["w_glu_val"].shape[0]
    xv = x.reshape(NB, sseg, d)
    tt = _pick(sseg, 64)

    uf, us, sa, sb = _in_proj(xv, p["norm_mix_pre"], p["w_in"], fw, sw, tt)
    ng = fw // LANES
    f = _fourier_mix(uf.reshape(ng, b, s, LANES)).reshape(ng, NB, sseg, LANES)

    tables = _ssm_tables(p["lam_re"], p["lam_im"], p["log_dt"], p["b_re"], p["b_im"],
                         p["c_re"], p["c_im"], p["d_skip"], sseg)
    ys = _ssm(us, tables, chained=(b == 1), mc=_pick(sseg // SCAN_BLOCK, 32))

    y = _mix_ffn(xv, f, ys, sa, sb, p["norm_mix_post"], p["norm_ffn_pre"], p["norm_ffn_post"],
                 p["w_fnet_out"], p["w_glu_val"], p["w_glu_gate"], p["w_out"],
                 p["w_ffn_gate"], p["w_ffn_up"], p["w_ffn_down"], tt)
    return y.reshape(b, s, d)


def _layer_params(l, norm_mix_pre, norm_mix_post, norm_ffn_pre, norm_ffn_post, w_in, w_fnet_out,
                  lam_re, lam_im, log_dt, b_re, b_im, c_re, c_im, d_skip, w_glu_val, w_glu_gate,
                  w_out, w_ffn_gate, w_ffn_up, w_ffn_down):
    row = lambda v: v[l].astype(F32).reshape(1, -1)
    w = lambda v: v[l].astype(BF16)
    return dict(
        norm_mix_pre=row(norm_mix_pre), norm_mix_post=row(norm_mix_post),
        norm_ffn_pre=row(norm_ffn_pre), norm_ffn_post=row(norm_ffn_post),
        w_in=w(w_in), w_fnet_out=w(w_fnet_out),
        lam_re=lam_re[l], lam_im=lam_im[l], log_dt=log_dt[l],
        b_re=b_re[l], b_im=b_im[l], c_re=c_re[l], c_im=c_im[l], d_skip=row(d_skip),
        w_glu_val=w(w_glu_val), w_glu_gate=w(w_glu_gate), w_out=w(w_out),
        w_ffn_gate=w(w_ffn_gate), w_ffn_up=w(w_ffn_up), w_ffn_down=w(w_ffn_down))


def kernel(x_prompt, x_sample, norm_mix_pre, norm_mix_post, norm_ffn_pre, norm_ffn_post, w_in, w_fnet_out, lam_re, lam_im, log_dt, b_re, b_im, c_re, c_im, d_skip, w_glu_val, w_glu_gate, w_out, w_ffn_gate, w_ffn_up, w_ffn_down):
    weights = (norm_mix_pre, norm_mix_post, norm_ffn_pre, norm_ffn_post, w_in, w_fnet_out,
               lam_re, lam_im, log_dt, b_re, b_im, c_re, c_im, d_skip, w_glu_val, w_glu_gate,
               w_out, w_ffn_gate, w_ffn_up, w_ffn_down)
    depth = w_in.shape[0]
    outs = []
    for x in (x_prompt, x_sample):
        for l in range(depth):
            x = _encoder_layer(x, _layer_params(l, *weights))
        outs.append(x)
    return tuple(outs)
```

```python
import functools
import math

import numpy as np
import jax
import jax.numpy as jnp
from jax import lax
from jax.experimental import pallas as pl
from jax.experimental.pallas import tpu as pltpu

F32 = jnp.float32
BF16 = jnp.bfloat16

EPS = 1e-6
SUBLANES = 8
LANES = 128
NB = SUBLANES

FNET_GROUP_DIM = 128
SSM_GROUP_DIM = 16
GROUPS_PER_BLOCK = LANES // SSM_GROUP_DIM
SCAN_BLOCK = LANES // SSM_GROUP_DIM
PAIR = 2
SCAN_INTERLEAVE = 4

FFT_N2 = 128
VMEM_LIMIT = 56 * 1024 * 1024


def _dot(a, b):
    return jnp.dot(a, b, preferred_element_type=F32)


def _rms(x, g):
    return x * lax.rsqrt(jnp.mean(x * x, axis=-1, keepdims=True) + EPS) * g


def _const_spec(shape):
    nd = len(shape)
    return pl.BlockSpec(shape, lambda *_: (0,) * nd, pipeline_mode=pl.Buffered(1))


def _in_proj_kernel(x_ref, g_ref, w_ref, uf_ref, us_ref, sa_ref, sb_ref, *, tt, fw, sw, d):
    rows = NB * tt
    x = x_ref[...].reshape(rows, d)
    h = _rms(x, g_ref[...]).astype(BF16)
    uf = _dot(h, w_ref[:, 0:fw])
    for g in range(fw // LANES):
        uf_ref[g] = uf[:, g * LANES:(g + 1) * LANES].reshape(NB, tt, LANES)
    us = _dot(h, w_ref[:, fw:fw + sw])
    for cb in range(sw // LANES):
        for b in range(NB):
            us_ref[cb, pl.ds(b, tt, stride=NB), :] = us[b * tt:(b + 1) * tt, cb * LANES:(cb + 1) * LANES]
    o = fw + sw
    sa_ref[...] = jax.nn.sigmoid(_dot(h, w_ref[:, o:o + d])).reshape(NB, tt, d).astype(BF16)
    sb_ref[...] = jax.nn.sigmoid(_dot(h, w_ref[:, o + d:o + 2 * d])).reshape(NB, tt, d).astype(BF16)


def _in_proj(xv, gain, w_in, fw, sw, tt):
    nb, sseg, d = xv.shape
    n = sseg // tt
    kern = functools.partial(_in_proj_kernel, tt=tt, fw=fw, sw=sw, d=d)
    return pl.pallas_call(
        kern,
        grid=(n,),
        in_specs=[
            pl.BlockSpec((NB, tt, d), lambda i: (0, i, 0)),
            _const_spec((1, d)),
            _const_spec(w_in.shape),
        ],
        out_specs=[
            pl.BlockSpec((fw // LANES, NB, tt, LANES), lambda i: (0, 0, i, 0)),
            pl.BlockSpec((sw // LANES, NB * tt, LANES), lambda i: (0, i, 0)),
            pl.BlockSpec((NB, tt, d), lambda i: (0, i, 0)),
            pl.BlockSpec((NB, tt, d), lambda i: (0, i, 0)),
        ],
        out_shape=[
            jax.ShapeDtypeStruct((fw // LANES, NB, sseg, LANES), F32),
            jax.ShapeDtypeStruct((sw // LANES, sseg * NB, LANES), F32),
            jax.ShapeDtypeStruct((NB, sseg, d), BF16),
            jax.ShapeDtypeStruct((NB, sseg, d), BF16),
        ],
        compiler_params=pltpu.CompilerParams(
            dimension_semantics=("arbitrary",), vmem_limit_bytes=VMEM_LIMIT),
        name="in_proj",
    )(xv, gain, w_in)


def _fft_tables(s):
    n2 = FFT_N2
    n1 = s // n2
    s1 = np.arange(n1, dtype=np.int64)
    k1 = np.arange(n1, dtype=np.int64)
    s2 = np.arange(n2, dtype=np.int64)
    idx = (k1[None, :, None] * s1[None, None, :] * n2 + s2[:, None, None] * k1[None, :, None]) % s
    ang = 2.0 * np.pi * idx.astype(np.float64) / s
    g = np.stack([np.cos(ang), -np.sin(ang)], axis=2) / math.sqrt(n1)
    g = g.reshape(n2, 2 * n1, n1)
    k2 = np.arange(n2, dtype=np.int64)
    ang2 = 2.0 * np.pi * ((k2[:, None] * s2[None, :]) % n2).astype(np.float64) / n2
    c2, s2m = np.cos(ang2), np.sin(ang2)
    h = np.block([[c2, s2m], [-s2m, c2]]) / math.sqrt(n2)
    c = np.arange(FNET_GROUP_DIM, dtype=np.int64)
    angc = 2.0 * np.pi * ((c[:, None] * c[None, :]) % FNET_GROUP_DIM).astype(np.float64) / FNET_GROUP_DIM
    cs = np.concatenate([np.cos(angc), np.sin(angc)], axis=0) / math.sqrt(FNET_GROUP_DIM)
    return g.astype(np.float32), h.astype(np.float32), cs.astype(np.float32)


def _fft1_kernel(x_ref, g_ref, o_ref, xs_scr, os_scr, *, n1, ng):
    xs_scr[...] = x_ref[...].reshape(ng, n1 * SUBLANES, LANES)
    for j in range(SUBLANES):
        xj = jnp.concatenate(
            [xs_scr[g, pl.ds(j, n1, stride=SUBLANES), :] for g in range(ng)], axis=1).astype(BF16)
        r = _dot(g_ref[j], xj)
        for g in range(ng):
            os_scr[g, pl.ds(j, 2 * n1, stride=SUBLANES), :] = r[:, g * LANES:(g + 1) * LANES]
    o_ref[...] = os_scr[...].reshape(ng, 2 * n1, SUBLANES, LANES)


def _fft2_kernel(a_ref, h_ref, cs_ref, o_ref, os_scr, *, k1blk, ng, n2):
    for j in range(k1blk):
        a = jnp.concatenate([a_ref[g, j] for g in range(ng)], axis=1).astype(BF16)
        y = _dot(h_ref[...], a).astype(BF16)
        for g in range(ng):
            yg = jnp.concatenate(
                [y[0:n2, g * LANES:(g + 1) * LANES], y[n2:2 * n2, g * LANES:(g + 1) * LANES]], axis=1)
            os_scr[g, pl.ds(j, n2, stride=k1blk), :] = _dot(yg, cs_ref[...])
    o_ref[...] = os_scr[...].reshape(ng, n2, k1blk, LANES)


def _fourier_mix(u, k1blk=SUBLANES):
    ng, b, s, c = u.shape
    assert c == FNET_GROUP_DIM == LANES
    n2 = FFT_N2
    n1 = s // n2
    assert n1 * n2 == s and n1 % k1blk == 0
    g_np, h_np, cs_np = _fft_tables(s)
    g_t = jnp.asarray(g_np).astype(BF16)
    h_t = jnp.asarray(h_np).astype(BF16)
    cs_t = jnp.asarray(cs_np).astype(BF16)

    a = pl.pallas_call(
        functools.partial(_fft1_kernel, n1=n1, ng=ng),
        grid=(b, n2 // SUBLANES),
        in_specs=[
            pl.BlockSpec((ng, None, n1, SUBLANES, c), lambda bi, i: (0, bi, 0, i, 0)),
            pl.BlockSpec((SUBLANES, 2 * n1, n1), lambda bi, i: (i, 0, 0)),
        ],
        out_specs=pl.BlockSpec((ng, None, 2 * n1, SUBLANES, c), lambda bi, i: (0, bi, 0, i, 0)),
        out_shape=jax.ShapeDtypeStruct((ng, b, 2 * n1, n2, c), F32),
        scratch_shapes=[pltpu.VMEM((ng, n1 * SUBLANES, c), F32),
                        pltpu.VMEM((ng, 2 * n1 * SUBLANES, c), F32)],
        compiler_params=pltpu.CompilerParams(
            dimension_semantics=("arbitrary", "arbitrary"), vmem_limit_bytes=VMEM_LIMIT),
        name="fft_stage1",
    )(u.reshape(ng, b, n1, n2, c), g_t)

    f = pl.pallas_call(
        functools.partial(_fft2_kernel, k1blk=k1blk, ng=ng, n2=n2),
        grid=(b, n1 // k1blk),
        in_specs=[
            pl.BlockSpec((ng, None, k1blk, 2 * n2, c), lambda bi, i: (0, bi, i, 0, 0)),
            _const_spec((2 * n2, 2 * n2)),
            _const_spec((2 * FNET_GROUP_DIM, FNET_GROUP_DIM)),
        ],
        out_specs=pl.BlockSpec((ng, None, n2, k1blk, c), lambda bi, i: (0, bi, 0, i, 0)),
        out_shape=jax.ShapeDtypeStruct((ng, b, n2, n1, c), F32),
        scratch_shapes=[pltpu.VMEM((ng, n2 * k1blk, c), F32)],
        compiler_params=pltpu.CompilerParams(
            dimension_semantics=("arbitrary", "arbitrary"), vmem_limit_bytes=VMEM_LIMIT),
        name="fft_stage2",
    )(a.reshape(ng, b, n1, 2 * n2, c), h_t, cs_t)
    return f.reshape(ng, b, s, c)


def _ssm_tables(lam_re, lam_im, log_dt, b_re, b_im, c_re, c_im, d_skip, seg_len):
    hp = lax.Precision.HIGHEST
    ndir, ng, p = lam_re.shape
    hd = b_re.shape[-1]
    r = SCAN_BLOCK
    nq = ng // PAIR
    lam_re = lam_re.astype(F32)
    lam_im = lam_im.astype(F32)
    dt = jnp.exp(log_dt.astype(F32))[..., None]

    def apow(k):
        mag = jnp.exp(lam_re * dt * k)
        return mag * jnp.cos(lam_im * dt * k), mag * jnp.sin(lam_im * dt * k)

    ar, ai = apow(1.0)
    den = lam_re * lam_re + lam_im * lam_im
    qr = ((ar - 1.0) * lam_re + ai * lam_im) / den
    qi = (ai * lam_re - (ar - 1.0) * lam_im) / den
    bre = qr[..., None] * b_re.astype(F32) - qi[..., None] * b_im.astype(F32)
    bim = qr[..., None] * b_im.astype(F32) + qi[..., None] * b_re.astype(F32)
    cr = c_re.astype(F32)
    ci = c_im.astype(F32)

    ks = jnp.arange(r + 1, dtype=F32).reshape(r + 1, 1, 1, 1)
    pr, pi = apow(ks)
    abr = pr[:r, ..., None] * bre - pi[:r, ..., None] * bim
    abi = pr[:r, ..., None] * bim + pi[:r, ..., None] * bre

    def in_rows(x):
        return jnp.transpose(x, (1, 0, 3, 2)).reshape(ng, r * hd, p)

    bp_re = jnp.stack([in_rows(abr[::-1, 0]), in_rows(abr[:, 1])])
    bp_im = jnp.stack([in_rows(abi[::-1, 0]), in_rows(abi[:, 1])])

    klag = (jnp.einsum('dgop,ldgph->ldgoh', cr, abr, precision=hp)
            - jnp.einsum('dgop,ldgph->ldgoh', ci, abi, precision=hp))
    jj = jnp.arange(r)[:, None]
    ii = jnp.arange(r)[None, :]
    kf = klag[:, 0][jnp.maximum(ii - jj, 0)] * (ii >= jj)[..., None, None, None]
    kb = klag[:, 1][jnp.maximum(jj - ii, 0)] * (jj >= ii)[..., None, None, None]
    kin = jnp.transpose(kf + kb, (2, 0, 4, 1, 3)).reshape(ng, r * hd, r * hd)
    dsk = d_skip.astype(F32).reshape(ng, hd)
    kin = kin + jnp.eye(r * hd, dtype=F32)[None] * jnp.tile(dsk, (1, r))[:, None, :]

    def state_out(d, pows):
        er = cr[d][None] * pr[pows, d][:, :, None, :] - ci[d][None] * pi[pows, d][:, :, None, :]
        ei = cr[d][None] * pi[pows, d][:, :, None, :] + ci[d][None] * pr[pows, d][:, :, None, :]
        to_rows = lambda e: jnp.transpose(e, (1, 3, 0, 2)).reshape(ng, p, r * hd)
        return to_rows(er), -to_rows(ei)

    cf_re, cf_im = state_out(0, jnp.arange(1, r + 1))
    cb_re, cb_im = state_out(1, r - jnp.arange(r))

    eye2 = jnp.eye(PAIR, dtype=F32)

    def pair_bd(m):
        lead = m.shape[:-3]
        rr, cc = m.shape[-2:]
        m = m.reshape(lead + (nq, PAIR, rr, cc))
        out = jnp.einsum('...qarc,ab->...qarbc', m, eye2, precision=hp)
        return out.reshape(lead + (nq, PAIR * rr, PAIR * cc))

    bq = jnp.concatenate([pair_bd(bp_re), pair_bd(bp_im)], axis=-1).astype(BF16)
    wy = jnp.concatenate([pair_bd(kin), pair_bd(cf_re), pair_bd(cf_im),
                          pair_bd(cb_re), pair_bd(cb_im)], axis=-2).astype(BF16)

    def lanes(v):
        v = v.reshape(ndir, nq, 1, PAIR * p)
        return jnp.broadcast_to(v, (ndir, nq, SUBLANES, PAIR * p))

    a8r, a8i = apow(float(r))
    alr, ali = apow(float(seg_len))
    return bq, wy, lanes(a8r), lanes(a8i), lanes(alr), lanes(ali)


def _transpose_pieces(vs):
    n = GROUPS_PER_BLOCK
    rows = vs[0].shape[0]
    piece = lax.broadcasted_iota(jnp.int32, (rows, LANES), 1) // SSM_GROUP_DIM
    cur = list(vs)
    d = 1
    while d < n:
        upper = (piece & d) != 0
        nxt = [None] * n
        for c in range(n):
            if c & d == 0:
                lo, hi = cur[c], cur[c | d]
                nxt[c] = jnp.where(upper, pltpu.roll(hi, d * SSM_GROUP_DIM, 1), lo)
                nxt[c | d] = jnp.where(upper, hi, pltpu.roll(lo, LANES - d * SSM_GROUP_DIM, 1))
        cur = nxt
        d *= 2
    return cur


def _scan_kernel(*refs, mc, nq, ncb, reverse, relayout, chain):
    if relayout:
        (src_ref, e_ref, bq_ref, a8re_ref, a8im_ref, alre_ref, alim_ref,
         u8_ref, x_ref, eo_ref, d_scr, st_scr) = refs
    else:
        (u8_ref, e_ref, bq_ref, a8re_ref, a8im_ref, alre_ref, alim_ref,
         x_ref, eo_ref, d_scr, st_scr) = refs
    i = pl.program_id(0)
    n = pl.num_programs(0)
    rows = mc * SUBLANES
    hl = LANES

    @pl.when(i == 0)
    def _init():
        if not chain:
            st_scr[...] = jnp.zeros(st_scr.shape, F32)
        else:
            row = lax.broadcasted_iota(jnp.int32, (SUBLANES, hl), 0)
            shift = SUBLANES - 1 if reverse else 1
            keep = (row <= SUBLANES - 2) if reverse else (row >= 1)
            for q in range(nq):
                er = pltpu.roll(e_ref[q, :, 0:hl], shift, 0)
                ei = pltpu.roll(e_ref[q, :, hl:2 * hl], shift, 0)
                lr = alre_ref[q]
                li = alim_ref[q]
                xr = jnp.zeros((SUBLANES, hl), F32)
                xi = jnp.zeros((SUBLANES, hl), F32)
                for _ in range(SUBLANES - 1):
                    pr = pltpu.roll(xr, shift, 0)
                    pi = pltpu.roll(xi, shift, 0)
                    xr = jnp.where(keep, lr * pr - li * pi + er, 0.0)
                    xi = jnp.where(keep, lr * pi + li * pr + ei, 0.0)
                st_scr[q, :, 0:hl] = xr
                st_scr[q, :, hl:2 * hl] = xi

    if relayout:
        for cb in range(ncb):
            vs = [src_ref[cb, :, j, :, :].reshape(rows, LANES) for j in range(SCAN_BLOCK)]
            folded = _transpose_pieces(vs)
            for gl in range(GROUPS_PER_BLOCK):
                g = cb * GROUPS_PER_BLOCK + gl
                u8_ref[g // PAIR, :, (g % PAIR) * LANES:(g % PAIR + 1) * LANES] = folded[gl].astype(BF16)

    for q0 in range(0, nq, SCAN_INTERLEAVE):
        qs = list(range(q0, min(q0 + SCAN_INTERLEAVE, nq)))
        for k, q in enumerate(qs):
            d_scr[k] = _dot(u8_ref[q], bq_ref[q])
        a_re = [a8re_ref[q] for q in qs]
        a_im = [a8im_ref[q] for q in qs]

        def body(step, carry, a_re=a_re, a_im=a_im, nk=len(qs)):
            m = (mc - 1 - step) if reverse else step
            r = pl.multiple_of(m * SUBLANES, SUBLANES)
            out = []
            for k in range(nk):
                re, im = carry[2 * k], carry[2 * k + 1]
                d_re = d_scr[k, pl.ds(r, SUBLANES), 0:hl]
                d_im = d_scr[k, pl.ds(r, SUBLANES), hl:2 * hl]
                d_scr[k, pl.ds(r, SUBLANES), 0:hl] = re
                d_scr[k, pl.ds(r, SUBLANES), hl:2 * hl] = im
                out.append(a_re[k] * re - a_im[k] * im + d_re)
                out.append(a_re[k] * im + a_im[k] * re + d_im)
            return tuple(out)

        init = []
        for q in qs:
            init += [st_scr[q, :, 0:hl], st_scr[q, :, hl:2 * hl]]
        fin = lax.fori_loop(0, mc, body, tuple(init), unroll=2)
        for k, q in enumerate(qs):
            st_scr[q, :, 0:hl] = fin[2 * k]
            st_scr[q, :, hl:2 * hl] = fin[2 * k + 1]
            x_ref[q] = d_scr[k].astype(BF16)

    @pl.when(i == n - 1)
    def _fin():
        eo_ref[...] = st_scr[...]


def _scan(src, e_in, bq, a8re, a8im, alre, alim, mc, reverse, relayout, chain):
    nq = bq.shape[0]
    if relayout:
        ncb, mtot = src.shape[0], src.shape[1]
    else:
        ncb, mtot = 0, src.shape[1] // NB
    n = mtot // mc
    rows = mc * NB
    order = (lambda i: n - 1 - i) if reverse else (lambda i: i)
    st_shape = (nq, SUBLANES, 2 * LANES)
    row_spec = pl.BlockSpec((nq, rows, 2 * LANES), lambda i: (0, order(i), 0))
    if relayout:
        src_spec = pl.BlockSpec((ncb, mc, SCAN_BLOCK, NB, LANES), lambda i: (0, order(i), 0, 0, 0))
    else:
        src_spec = row_spec
    out_specs = [row_spec, _const_spec(st_shape)]
    out_shape = [jax.ShapeDtypeStruct((nq, mtot * NB, 2 * LANES), BF16),
                 jax.ShapeDtypeStruct(st_shape, F32)]
    if relayout:
        out_specs = [row_spec] + out_specs
        out_shape = [jax.ShapeDtypeStruct((nq, mtot * NB, 2 * LANES), BF16)] + out_shape
    kern = functools.partial(_scan_kernel, mc=mc, nq=nq, ncb=ncb, reverse=reverse,
                             relayout=relayout, chain=chain)
    return pl.pallas_call(
        kern,
        grid=(n,),
        in_specs=[src_spec, _const_spec(st_shape), _const_spec(bq.shape),
                  _const_spec(a8re.shape), _const_spec(a8im.shape),
                  _const_spec(alre.shape), _const_spec(alim.shape)],
        out_specs=out_specs,
        out_shape=out_shape,
        scratch_shapes=[pltpu.VMEM((SCAN_INTERLEAVE, rows, 2 * LANES), F32),
                        pltpu.VMEM(st_shape, F32)],
        compiler_params=pltpu.CompilerParams(
            dimension_semantics=("arbitrary",), vmem_limit_bytes=VMEM_LIMIT),
        name="ssm_scan_" + ("bwd" if reverse else "fwd") + ("_chain" if chain else ""),
    )(src, e_in, bq, a8re, a8im, alre, alim)


def _ssm_out_kernel(u8_ref, xf_ref, xb_ref, wy_ref, y_ref, ys_scr, *, mc, ncb):
    rows = mc * SUBLANES
    qpb = GROUPS_PER_BLOCK // PAIR
    for cb in range(ncb):
        for ql in range(qpb):
            q = cb * qpb + ql
            lhs = jnp.concatenate([u8_ref[q], xf_ref[q], xb_ref[q]], axis=1)
            y8 = _dot(lhs, wy_ref[q])
            for g2 in range(PAIR):
                ys_scr[PAIR * ql + g2] = y8[:, g2 * LANES:(g2 + 1) * LANES]
        unfolded = _transpose_pieces([ys_scr[gl] for gl in range(GROUPS_PER_BLOCK)])
        for i in range(SCAN_BLOCK):
            y_ref[cb, :, i, :, :] = unfolded[i].reshape(mc, NB, LANES)


def _ssm_out(u8, xf, xb, wy, ncb, mc):
    nq, rows_total, _ = u8.shape
    mtot = rows_total // NB
    n = mtot // mc
    rows = mc * NB
    row_spec = pl.BlockSpec((nq, rows, 2 * LANES), lambda i: (0, i, 0))
    return pl.pallas_call(
        functools.partial(_ssm_out_kernel, mc=mc, ncb=ncb),
        grid=(n,),
        in_specs=[row_spec, row_spec, row_spec, _const_spec(wy.shape)],
        out_specs=pl.BlockSpec((ncb, mc, SCAN_BLOCK, NB, LANES), lambda i: (0, i, 0, 0, 0)),
        out_shape=jax.ShapeDtypeStruct((ncb, mtot, SCAN_BLOCK, NB, LANES), F32),
        scratch_shapes=[pltpu.VMEM((GROUPS_PER_BLOCK, rows, LANES), F32)],
        compiler_params=pltpu.CompilerParams(
            dimension_semantics=("arbitrary",), vmem_limit_bytes=VMEM_LIMIT),
        name="ssm_out",
    )(u8, xf, xb, wy)


def _ssm(us, tables, chained, mc):
    bq, wy, a8re, a8im, alre, alim = tables
    ncb, rows_tb, _ = us.shape
    sseg = rows_tb // NB
    mtot = sseg // SCAN_BLOCK
    nq = bq.shape[1]
    e0 = jnp.zeros((nq, SUBLANES, 2 * LANES), F32)
    src = us.reshape(ncb, mtot, SCAN_BLOCK, NB, LANES)
    fwd = lambda s, e, relayout, chain: _scan(
        s, e, bq[0], a8re[0], a8im[0], alre[0], alim[0], mc, False, relayout, chain)
    bwd = lambda s, e, chain: _scan(
        s, e, bq[1], a8re[1], a8im[1], alre[1], alim[1], mc, True, False, chain)
    u8, xf, ef = fwd(src, e0, True, False)
    xb, eb = bwd(u8, e0, False)
    if chained:
        xf, _ = fwd(u8, ef, False, True)
        xb, _ = bwd(u8, eb, True)
    y = _ssm_out(u8, xf, xb, wy, ncb, mc)
    return y.reshape(ncb, rows_tb, LANES)


def _ffn_chunks(hidden, step=1024):
    edges = list(range(0, hidden, step)) + [hidden]
    return list(zip(edges[:-1], edges[1:]))


def _mix_ffn_kernel(x_ref, f_ref, y_ref, sa_ref, sb_ref, gpost_ref, gpre_ref, gfpost_ref,
                    wfo_ref, wval_ref, wgate_ref, wout_ref, wfg_ref, wfu_ref, wfd_ref, o_ref,
                    *, tt, d, fw, sw, hidden):
    rows = NB * tt
    ys = jnp.concatenate(
        [jnp.concatenate(
            [y_ref[cb, pl.ds(b, tt, stride=NB), :] for b in range(NB)], axis=0)
         for cb in range(sw // LANES)], axis=1)
    z = jax.nn.gelu(ys).astype(BF16)
    f = jnp.concatenate(
        [f_ref[g].reshape(rows, LANES) for g in range(fw // LANES)], axis=1).astype(BF16)
    br_a = _dot(f, wfo_ref[...])
    br_b = _dot(z, wval_ref[...]) * jax.nn.sigmoid(_dot(z, wgate_ref[...]))
    sa = sa_ref[...].reshape(rows, d).astype(F32)
    sb = sb_ref[...].reshape(rows, d).astype(F32)
    merged = (sa * br_a + sb * br_b).astype(BF16)
    m = _dot(merged, wout_ref[...])
    x1 = x_ref[...].reshape(rows, d) + _rms(m, gpost_ref[...])
    h2 = _rms(x1, gpre_ref[...]).astype(BF16)
    acc = None
    for c0, c1 in _ffn_chunks(hidden):
        g = _dot(h2, wfg_ref[:, c0:c1])
        u = _dot(h2, wfu_ref[:, c0:c1])
        part = _dot((jax.nn.silu(g) * u).astype(BF16), wfd_ref[c0:c1, :])
        acc = part if acc is None else acc + part
    o_ref[...] = (x1 + _rms(acc, gfpost_ref[...])).reshape(NB, tt, d)


def _mix_ffn(xv, f, y, sa, sb, gpost, gpre, gfpost, wfo, wval, wgate, wout, wfg, wfu, wfd, tt):
    nb, sseg, d = xv.shape
    fw = f.shape[0] * LANES
    nblk = y.shape[0]
    sw = nblk * LANES
    hidden = wfg.shape[-1]
    n = sseg // tt
    kern = functools.partial(_mix_ffn_kernel, tt=tt, d=d, fw=fw, sw=sw, hidden=hidden)
    tok = lambda width: pl.BlockSpec((NB, tt, width), lambda i: (0, i, 0))
    tmaj = pl.BlockSpec((nblk, NB * tt, LANES), lambda i: (0, i, 0))
    return pl.pallas_call(
        kern,
        grid=(n,),
        in_specs=[tok(d), pl.BlockSpec((fw // LANES, NB, tt, LANES), lambda i: (0, 0, i, 0)),
                  tmaj, tok(d), tok(d),
                  _const_spec((1, d)), _const_spec((1, d)), _const_spec((1, d)),
                  _const_spec(wfo.shape), _const_spec(wval.shape), _const_spec(wgate.shape),
                  _const_spec(wout.shape), _const_spec(wfg.shape), _const_spec(wfu.shape),
                  _const_spec(wfd.shape)],
        out_specs=tok(d),
        out_shape=jax.ShapeDtypeStruct((NB, sseg, d), F32),
        compiler_params=pltpu.CompilerParams(
            dimension_semantics=("arbitrary",), vmem_limit_bytes=VMEM_LIMIT),
        name="mix_ffn",
    )(xv, f, y, sa, sb, gpost, gpre, gfpost, wfo, wval, wgate, wout, wfg, wfu, wfd)


def _pick(total, want):
    t = min(want, total)
    assert total % t == 0, (total, t)
    return t


def _encoder_layer(x, p):
    b, s, d = x.shape
    assert b in (1, NB), "one sequence (split in NB segments) or NB sequences"
    sseg = (b * s) // NB
    fw = p["w_fnet_out"].shape[0]
    sw = p["w_glu_val"].shape[0]
    xv = x.reshape(NB, sseg, d)
    tt = _pick(sseg, 64)

    uf, us, sa, sb = _in_proj(xv, p["norm_mix_pre"], p["w_in"], fw, sw, tt)
    ng = fw // LANES
    f = _fourier_mix(uf.reshape(ng, b, s, LANES)).reshape(ng, NB, sseg, LANES)

    tables = _ssm_tables(p["lam_re"], p["lam_im"], p["log_dt"], p["b_re"], p["b_im"],
                         p["c_re"], p["c_im"], p["d_skip"], sseg)
    ys = _ssm(us, tables, chained=(b == 1), mc=_pick(sseg // SCAN_BLOCK, 32))

    y = _mix_ffn(xv, f, ys, sa, sb, p["norm_mix_post"], p["norm_ffn_pre"], p["norm_ffn_post"],
                 p["w_fnet_out"], p["w_glu_val"], p["w_glu_gate"], p["w_out"],
                 p["w_ffn_gate"], p["w_ffn_up"], p["w_ffn_down"], tt)
    return y.reshape(b, s, d)


def _layer_params(l, norm_mix_pre, norm_mix_post, norm_ffn_pre, norm_ffn_post, w_in, w_fnet_out,
                  lam_re, lam_im, log_dt, b_re, b_im, c_re, c_im, d_skip, w_glu_val, w_glu_gate,
                  w_out, w_ffn_gate, w_ffn_up, w_ffn_down):
    row = lambda v: v[l].astype(F32).reshape(1, -1)
    w = lambda v: v[l].astype(BF16)
    return dict(
        norm_mix_pre=row(norm_mix_pre), norm_mix_post=row(norm_mix_post),
        norm_ffn_pre=row(norm_ffn_pre), norm_ffn_post=row(norm_ffn_post),
        w_in=w(w_in), w_fnet_out=w(w_fnet_out),
        lam_re=lam_re[l], lam_im=lam_im[l], log_dt=log_dt[l],
        b_re=b_re[l], b_im=b_im[l], c_re=c_re[l], c_im=c_im[l], d_skip=row(d_skip),
        w_glu_val=w(w_glu_val), w_glu_gate=w(w_glu_gate), w_out=w(w_out),
        w_ffn_gate=w(w_ffn_gate), w_ffn_up=w(w_ffn_up), w_ffn_down=w(w_ffn_down))


def kernel(x_prompt, x_sample, norm_mix_pre, norm_mix_post, norm_ffn_pre, norm_ffn_post, w_in, w_fnet_out, lam_re, lam_im, log_dt, b_re, b_im, c_re, c_im, d_skip, w_glu_val, w_glu_gate, w_out, w_ffn_gate, w_ffn_up, w_ffn_down):
    weights = (norm_mix_pre, norm_mix_post, norm_ffn_pre, norm_ffn_post, w_in, w_fnet_out,
               lam_re, lam_im, log_dt, b_re, b_im, c_re, c_im, d_skip, w_glu_val, w_glu_gate,
               w_out, w_ffn_gate, w_ffn_up, w_ffn_down)
    depth = w_in.shape[0]
    outs = []
    for x in (x_prompt, x_sample):
        for l in range(depth):
            x = _encoder_layer(x, _layer_params(l, *weights))
        outs.append(x)
    return tuple(outs)
```

```python
import functools
import math

import numpy as np
import jax
import jax.numpy as jnp
from jax import lax
from jax.experimental import pallas as pl
from jax.experimental.pallas import tpu as pltpu

F32 = jnp.float32
BF16 = jnp.bfloat16

EPS = 1e-6
SUBLANES = 8
LANES = 128
NB = SUBLANES

FNET_GROUP_DIM = 128
SSM_GROUP_DIM = 16
GROUPS_PER_BLOCK = LANES // SSM_GROUP_DIM
SCAN_BLOCK = LANES // SSM_GROUP_DIM
PAIR = 2
SCAN_INTERLEAVE = 4

FFT_N2 = 128
FFT_S2_BLOCK = 16
FFT_K1_BLOCK = 8
FFT_GROUPS_PER_PASS = 2
FFT_ROW_PAD = SUBLANES
VMEM_LIMIT = 56 * 1024 * 1024


def _dot(a, b):
    return jnp.dot(a, b, preferred_element_type=F32)


def _rms(x, g):
    return x * lax.rsqrt(jnp.mean(x * x, axis=-1, keepdims=True) + EPS) * g


def _const_spec(shape):
    nd = len(shape)
    return pl.BlockSpec(shape, lambda *_: (0,) * nd, pipeline_mode=pl.Buffered(1))


def _in_proj_kernel(x_ref, g_ref, w_ref, uf_ref, us_ref, sa_ref, sb_ref, *, tt, fw, sw, d):
    rows = NB * tt
    x = x_ref[...].reshape(rows, d)
    h = _rms(x, g_ref[...]).astype(BF16)
    uf = _dot(h, w_ref[:, 0:fw])
    for g in range(fw // LANES):
        uf_ref[g] = uf[:, g * LANES:(g + 1) * LANES].reshape(NB, tt, LANES)
    us = _dot(h, w_ref[:, fw:fw + sw])
    for cb in range(sw // LANES):
        for b in range(NB):
            us_ref[cb, pl.ds(b, tt, stride=NB), :] = us[b * tt:(b + 1) * tt, cb * LANES:(cb + 1) * LANES]
    o = fw + sw
    sa_ref[...] = jax.nn.sigmoid(_dot(h, w_ref[:, o:o + d])).reshape(NB, tt, d).astype(BF16)
    sb_ref[...] = jax.nn.sigmoid(_dot(h, w_ref[:, o + d:o + 2 * d])).reshape(NB, tt, d).astype(BF16)


def _in_proj(xv, gain, w_in, fw, sw, tt):
    nb, sseg, d = xv.shape
    n = sseg // tt
    kern = functools.partial(_in_proj_kernel, tt=tt, fw=fw, sw=sw, d=d)
    return pl.pallas_call(
        kern,
        grid=(n,),
        in_specs=[
            pl.BlockSpec((NB, tt, d), lambda i: (0, i, 0)),
            _const_spec((1, d)),
            _const_spec(w_in.shape),
        ],
        out_specs=[
            pl.BlockSpec((fw // LANES, NB, tt, LANES), lambda i: (0, 0, i, 0)),
            pl.BlockSpec((sw // LANES, NB * tt, LANES), lambda i: (0, i, 0)),
            pl.BlockSpec((NB, tt, d), lambda i: (0, i, 0)),
            pl.BlockSpec((NB, tt, d), lambda i: (0, i, 0)),
        ],
        out_shape=[
            jax.ShapeDtypeStruct((fw // LANES, NB, sseg, LANES), F32),
            jax.ShapeDtypeStruct((sw // LANES, sseg * NB, LANES), F32),
            jax.ShapeDtypeStruct((NB, sseg, d), BF16),
            jax.ShapeDtypeStruct((NB, sseg, d), BF16),
        ],
        compiler_params=pltpu.CompilerParams(
            dimension_semantics=("arbitrary",), vmem_limit_bytes=VMEM_LIMIT),
        name="in_proj",
    )(xv, gain, w_in)


def _fft_tables(s):
    n2 = FFT_N2
    n1 = s // n2
    s1 = np.arange(n1, dtype=np.int64)
    k1 = np.arange(n1, dtype=np.int64)
    s2 = np.arange(n2, dtype=np.int64)
    idx = (k1[None, :, None] * s1[None, None, :] * n2 + s2[:, None, None] * k1[None, :, None]) % s
    ang = 2.0 * np.pi * idx.astype(np.float64) / s
    g = np.stack([np.cos(ang), -np.sin(ang)], axis=1) / math.sqrt(n1)
    g = g.reshape(n2, 2 * n1, n1)
    k2 = np.arange(n2, dtype=np.int64)
    ang2 = 2.0 * np.pi * ((k2[:, None] * s2[None, :]) % n2).astype(np.float64) / n2
    c2, s2m = np.cos(ang2), np.sin(ang2)
    h = np.block([[c2, s2m], [-s2m, c2]]) / math.sqrt(n2)
    c = np.arange(FNET_GROUP_DIM, dtype=np.int64)
    angc = 2.0 * np.pi * ((c[:, None] * c[None, :]) % FNET_GROUP_DIM).astype(np.float64) / FNET_GROUP_DIM
    cs = np.concatenate([np.cos(angc), np.sin(angc)], axis=0) / math.sqrt(FNET_GROUP_DIM)
    return g.astype(np.float32), h.astype(np.float32), cs.astype(np.float32)


def _fft_kernel(x_ref, g_ref, h_ref, cs_ref, o_ref, xs_scr, a_scr, os_scr, *, n1, n2, ng, p1):
    p = pl.program_id(2)
    s2b, k1b = FFT_S2_BLOCK, FFT_K1_BLOCK
    kstride = 2 * n2 + FFT_ROW_PAD

    @pl.when(p < p1)
    def _stage1():
        xs_scr[...] = x_ref[...].reshape(ng, n1 * s2b, LANES)
        for j in range(s2b):
            xj = jnp.concatenate(
                [xs_scr[g, pl.ds(j, n1, stride=s2b), :] for g in range(ng)], axis=1).astype(BF16)
            r = _dot(g_ref[j], xj)
            s2 = p * s2b + j
            for g in range(ng):
                a_scr[g, pl.ds(s2, n1, stride=kstride), :] = r[0:n1, g * LANES:(g + 1) * LANES]
                a_scr[g, pl.ds(n2 + s2, n1, stride=kstride), :] = r[n1:2 * n1, g * LANES:(g + 1) * LANES]

    @pl.when(p >= p1)
    def _stage2():
        for j in range(k1b):
            row0 = pl.multiple_of(((p - p1) * k1b + j) * kstride, SUBLANES)
            a = jnp.concatenate(
                [a_scr[g, pl.ds(row0, 2 * n2), :] for g in range(ng)], axis=1).astype(BF16)
            y = _dot(h_ref[...], a).astype(BF16)
            for g in range(ng):
                yg = jnp.concatenate(
                    [y[0:n2, g * LANES:(g + 1) * LANES], y[n2:2 * n2, g * LANES:(g + 1) * LANES]], axis=1)
                os_scr[g, pl.ds(j, n2, stride=k1b), :] = _dot(yg, cs_ref[...])
        o_ref[...] = os_scr[...].reshape(ng, n2, k1b, LANES)


def _fourier_mix(u):
    ngroups, b, s, c = u.shape
    assert c == FNET_GROUP_DIM == LANES
    n2 = FFT_N2
    n1 = s // n2
    ng = FFT_GROUPS_PER_PASS
    s2b, k1b = FFT_S2_BLOCK, FFT_K1_BLOCK
    assert n1 * n2 == s and n1 % k1b == 0 and n2 % s2b == 0 and ngroups % ng == 0
    p1, p2 = n2 // s2b, n1 // k1b
    kstride = 2 * n2 + FFT_ROW_PAD
    g_np, h_np, cs_np = _fft_tables(s)
    g_t = jnp.asarray(g_np).astype(BF16)
    h_t = jnp.asarray(h_np).astype(BF16)
    cs_t = jnp.asarray(cs_np).astype(BF16)

    f = pl.pallas_call(
        functools.partial(_fft_kernel, n1=n1, n2=n2, ng=ng, p1=p1),
        grid=(b, ngroups // ng, p1 + p2),
        in_specs=[
            pl.BlockSpec((ng, None, n1, s2b, c),
                         lambda bi, gi, p: (gi, bi, 0, jnp.minimum(p, p1 - 1), 0)),
            pl.BlockSpec((s2b, 2 * n1, n1), lambda bi, gi, p: (jnp.minimum(p, p1 - 1), 0, 0)),
            _const_spec((2 * n2, 2 * n2)),
            _const_spec((2 * FNET_GROUP_DIM, FNET_GROUP_DIM)),
        ],
        out_specs=pl.BlockSpec((ng, None, n2, k1b, c),
                               lambda bi, gi, p: (gi, bi, 0, jnp.maximum(p - p1, 0), 0)),
        out_shape=jax.ShapeDtypeStruct((ngroups, b, n2, n1, c), F32),
        scratch_shapes=[pltpu.VMEM((ng, n1 * s2b, c), F32),
                        pltpu.VMEM((ng, n1 * kstride, c), F32),
                        pltpu.VMEM((ng, n2 * k1b, c), F32)],
        compiler_params=pltpu.CompilerParams(
            dimension_semantics=("arbitrary", "arbitrary", "arbitrary"),
            vmem_limit_bytes=VMEM_LIMIT),
        name="fft",
    )(u.reshape(ngroups, b, n1, n2, c), g_t, h_t, cs_t)
    return f.reshape(ngroups, b, s, c)


def _ssm_tables(lam_re, lam_im, log_dt, b_re, b_im, c_re, c_im, d_skip, seg_len):
    assert PAIR == 2
    ndir, ng, p = lam_re.shape
    hd = b_re.shape[-1]
    r = SCAN_BLOCK
    nq = ng // PAIR
    lam_re = lam_re.astype(F32)
    lam_im = lam_im.astype(F32)
    dt = jnp.exp(log_dt.astype(F32))[..., None]

    def apow(k):
        mag = jnp.exp(lam_re * dt * k)
        return mag * jnp.cos(lam_im * dt * k), mag * jnp.sin(lam_im * dt * k)

    ar, ai = apow(1.0)
    den = lam_re * lam_re + lam_im * lam_im
    qr = ((ar - 1.0) * lam_re + ai * lam_im) / den
    qi = (ai * lam_re - (ar - 1.0) * lam_im) / den
    bre = qr[..., None] * b_re.astype(F32) - qi[..., None] * b_im.astype(F32)
    bim = qr[..., None] * b_im.astype(F32) + qi[..., None] * b_re.astype(F32)
    cr = c_re.astype(F32)
    ci = c_im.astype(F32)

    ks = jnp.arange(r + 1, dtype=F32).reshape(r + 1, 1, 1, 1)
    pr, pi = apow(ks)
    abr = pr[:r, ..., None] * bre - pi[:r, ..., None] * bim
    abi = pr[:r, ..., None] * bim + pi[:r, ..., None] * bre

    def in_rows(x):
        return jnp.transpose(x, (1, 0, 3, 2)).reshape(ng, r * hd, p)

    bp_re = jnp.stack([in_rows(abr[::-1, 0]), in_rows(abr[:, 1])])
    bp_im = jnp.stack([in_rows(abi[::-1, 0]), in_rows(abi[:, 1])])

    klag = jnp.sum(cr[None, :, :, :, :, None] * abr[:, :, :, None, :, :]
                   - ci[None, :, :, :, :, None] * abi[:, :, :, None, :, :], axis=4)
    kzero = jnp.zeros_like(klag[0, 0])
    kin = jnp.stack([
        jnp.stack([(klag[i - j, 0] if i >= j else kzero) + (klag[j - i, 1] if j >= i else kzero)
                   for i in range(r)]) for j in range(r)])
    kin = jnp.transpose(kin, (2, 0, 4, 1, 3)).reshape(ng, r * hd, r * hd)
    dsk = d_skip.astype(F32).reshape(ng, hd)
    kin = kin + jnp.eye(r * hd, dtype=F32)[None] * jnp.tile(dsk, (1, r))[:, None, :]

    def state_out(d, pows):
        er = cr[d][None] * pr[pows, d][:, :, None, :] - ci[d][None] * pi[pows, d][:, :, None, :]
        ei = cr[d][None] * pi[pows, d][:, :, None, :] + ci[d][None] * pr[pows, d][:, :, None, :]
        to_rows = lambda e: jnp.transpose(e, (1, 3, 0, 2)).reshape(ng, p, r * hd)
        return to_rows(er), -to_rows(ei)

    cf_re, cf_im = state_out(0, jnp.arange(1, r + 1))
    cb_re, cb_im = state_out(1, r - jnp.arange(r))

    def pair_bd(m):
        lead = m.shape[:-3]
        rr, cc = m.shape[-2:]
        m = m.reshape(lead + (nq, PAIR, rr, cc))
        m0, m1 = m[..., 0, :, :], m[..., 1, :, :]
        z = jnp.zeros_like(m0)
        return jnp.concatenate([jnp.concatenate([m0, z], axis=-1),
                                jnp.concatenate([z, m1], axis=-1)], axis=-2)

    bq = jnp.concatenate([pair_bd(bp_re), pair_bd(bp_im)], axis=-1).astype(BF16)
    wy = jnp.concatenate([pair_bd(kin), pair_bd(cf_re), pair_bd(cf_im),
                          pair_bd(cb_re), pair_bd(cb_im)], axis=-2).astype(BF16)

    def lanes(v):
        v = v.reshape(ndir, nq, 1, PAIR * p)
        return jnp.broadcast_to(v, (ndir, nq, SUBLANES, PAIR * p))

    a8r, a8i = apow(float(r))
    alr, ali = apow(float(seg_len))
    return bq, wy, lanes(a8r), lanes(a8i), lanes(alr), lanes(ali)


def _transpose_pieces(vs):
    n = GROUPS_PER_BLOCK
    rows = vs[0].shape[0]
    piece = lax.broadcasted_iota(jnp.int32, (rows, LANES), 1) // SSM_GROUP_DIM
    cur = list(vs)
    d = 1
    while d < n:
        upper = (piece & d) != 0
        nxt = [None] * n
        for c in range(n):
            if c & d == 0:
                lo, hi = cur[c], cur[c | d]
                nxt[c] = jnp.where(upper, pltpu.roll(hi, d * SSM_GROUP_DIM, 1), lo)
                nxt[c | d] = jnp.where(upper, hi, pltpu.roll(lo, LANES - d * SSM_GROUP_DIM, 1))
        cur = nxt
        d *= 2
    return cur


def _scan_kernel(*refs, mc, nq, ncb, reverse, relayout, chain):
    if relayout:
        (src_ref, e_ref, bq_ref, a8re_ref, a8im_ref, alre_ref, alim_ref,
         u8_ref, x_ref, eo_ref, d_scr, st_scr) = refs
    else:
        (u8_ref, e_ref, bq_ref, a8re_ref, a8im_ref, alre_ref, alim_ref,
         x_ref, eo_ref, d_scr, st_scr) = refs
    i = pl.program_id(0)
    n = pl.num_programs(0)
    rows = mc * SUBLANES
    hl = LANES

    @pl.when(i == 0)
    def _init():
        if not chain:
            st_scr[...] = jnp.zeros(st_scr.shape, F32)
        else:
            row = lax.broadcasted_iota(jnp.int32, (SUBLANES, hl), 0)
            shift = SUBLANES - 1 if reverse else 1
            keep = (row <= SUBLANES - 2) if reverse else (row >= 1)
            for q in range(nq):
                er = pltpu.roll(e_ref[q, :, 0:hl], shift, 0)
                ei = pltpu.roll(e_ref[q, :, hl:2 * hl], shift, 0)
                lr = alre_ref[q]
                li = alim_ref[q]
                xr = jnp.zeros((SUBLANES, hl), F32)
                xi = jnp.zeros((SUBLANES, hl), F32)
                for _ in range(SUBLANES - 1):
                    pr = pltpu.roll(xr, shift, 0)
                    pi = pltpu.roll(xi, shift, 0)
                    xr = jnp.where(keep, lr * pr - li * pi + er, 0.0)
                    xi = jnp.where(keep, lr * pi + li * pr + ei, 0.0)
                st_scr[q, :, 0:hl] = xr
                st_scr[q, :, hl:2 * hl] = xi

    if relayout:
        for cb in range(ncb):
            vs = [src_ref[cb, :, j, :, :].reshape(rows, LANES) for j in range(SCAN_BLOCK)]
            folded = _transpose_pieces(vs)
            for gl in range(GROUPS_PER_BLOCK):
                g = cb * GROUPS_PER_BLOCK + gl
                u8_ref[g // PAIR, :, (g % PAIR) * LANES:(g % PAIR + 1) * LANES] = folded[gl].astype(BF16)

    for q0 in range(0, nq, SCAN_INTERLEAVE):
        qs = list(range(q0, min(q0 + SCAN_INTERLEAVE, nq)))
        for k, q in enumerate(qs):
            d_scr[k] = _dot(u8_ref[q], bq_ref[q])
        a_re = [a8re_ref[q] for q in qs]
        a_im = [a8im_ref[q] for q in qs]

        def body(step, carry, a_re=a_re, a_im=a_im, nk=len(qs)):
            m = (mc - 1 - step) if reverse else step
            r = pl.multiple_of(m * SUBLANES, SUBLANES)
            out = []
            for k in range(nk):
                re, im = carry[2 * k], carry[2 * k + 1]
                d_re = d_scr[k, pl.ds(r, SUBLANES), 0:hl]
                d_im = d_scr[k, pl.ds(r, SUBLANES), hl:2 * hl]
                d_scr[k, pl.ds(r, SUBLANES), 0:hl] = re
                d_scr[k, pl.ds(r, SUBLANES), hl:2 * hl] = im
                out.append(a_re[k] * re - a_im[k] * im + d_re)
                out.append(a_re[k] * im + a_im[k] * re + d_im)
            return tuple(out)

        init = []
        for q in qs:
            init += [st_scr[q, :, 0:hl], st_scr[q, :, hl:2 * hl]]
        fin = lax.fori_loop(0, mc, body, tuple(init), unroll=2)
        for k, q in enumerate(qs):
            st_scr[q, :, 0:hl] = fin[2 * k]
            st_scr[q, :, hl:2 * hl] = fin[2 * k + 1]
            x_ref[q] = d_scr[k].astype(BF16)

    @pl.when(i == n - 1)
    def _fin():
        eo_ref[...] = st_scr[...]


def _scan(src, e_in, bq, a8re, a8im, alre, alim, mc, reverse, relayout, chain):
    nq = bq.shape[0]
    if relayout:
        ncb, mtot = src.shape[0], src.shape[1]
    else:
        ncb, mtot = 0, src.shape[1] // NB
    n = mtot // mc
    rows = mc * NB
    order = (lambda i: n - 1 - i) if reverse else (lambda i: i)
    st_shape = (nq, SUBLANES, 2 * LANES)
    row_spec = pl.BlockSpec((nq, rows, 2 * LANES), lambda i: (0, order(i), 0))
    if relayout:
        src_spec = pl.BlockSpec((ncb, mc, SCAN_BLOCK, NB, LANES), lambda i: (0, order(i), 0, 0, 0))
    else:
        src_spec = row_spec
    out_specs = [row_spec, _const_spec(st_shape)]
    out_shape = [jax.ShapeDtypeStruct((nq, mtot * NB, 2 * LANES), BF16),
                 jax.ShapeDtypeStruct(st_shape, F32)]
    if relayout:
        out_specs = [row_spec] + out_specs
        out_shape = [jax.ShapeDtypeStruct((nq, mtot * NB, 2 * LANES), BF16)] + out_shape
    kern = functools.partial(_scan_kernel, mc=mc, nq=nq, ncb=ncb, reverse=reverse,
                             relayout=relayout, chain=chain)
    return pl.pallas_call(
        kern,
        grid=(n,),
        in_specs=[src_spec, _const_spec(st_shape), _const_spec(bq.shape),
                  _const_spec(a8re.shape), _const_spec(a8im.shape),
                  _const_spec(alre.shape), _const_spec(alim.shape)],
        out_specs=out_specs,
        out_shape=out_shape,
        scratch_shapes=[pltpu.VMEM((SCAN_INTERLEAVE, rows, 2 * LANES), F32),
                        pltpu.VMEM(st_shape, F32)],
        compiler_params=pltpu.CompilerParams(
            dimension_semantics=("arbitrary",), vmem_limit_bytes=VMEM_LIMIT),
        name="ssm_scan_" + ("bwd" if reverse else "fwd") + ("_chain" if chain else ""),
    )(src, e_in, bq, a8re, a8im, alre, alim)


def _ssm_out_kernel(u8_ref, xf_ref, xb_ref, wy_ref, y_ref, ys_scr, *, mc, ncb):
    rows = mc * SUBLANES
    qpb = GROUPS_PER_BLOCK // PAIR
    for cb in range(ncb):
        for ql in range(qpb):
            q = cb * qpb + ql
            lhs = jnp.concatenate([u8_ref[q], xf_ref[q], xb_ref[q]], axis=1)
            y8 = _dot(lhs, wy_ref[q])
            for g2 in range(PAIR):
                ys_scr[PAIR * ql + g2] = y8[:, g2 * LANES:(g2 + 1) * LANES]
        unfolded = _transpose_pieces([ys_scr[gl] for gl in range(GROUPS_PER_BLOCK)])
        for i in range(SCAN_BLOCK):
            y_ref[cb, :, i, :, :] = unfolded[i].reshape(mc, NB, LANES)


def _ssm_out(u8, xf, xb, wy, ncb, mc):
    nq, rows_total, _ = u8.shape
    mtot = rows_total // NB
    n = mtot // mc
    rows = mc * NB
    row_spec = pl.BlockSpec((nq, rows, 2 * LANES), lambda i: (0, i, 0))
    return pl.pallas_call(
        functools.partial(_ssm_out_kernel, mc=mc, ncb=ncb),
        grid=(n,),
        in_specs=[row_spec, row_spec, row_spec, _const_spec(wy.shape)],
        out_specs=pl.BlockSpec((ncb, mc, SCAN_BLOCK, NB, LANES), lambda i: (0, i, 0, 0, 0)),
        out_shape=jax.ShapeDtypeStruct((ncb, mtot, SCAN_BLOCK, NB, LANES), F32),
        scratch_shapes=[pltpu.VMEM((GROUPS_PER_BLOCK, rows, LANES), F32)],
        compiler_params=pltpu.CompilerParams(
            dimension_semantics=("arbitrary",), vmem_limit_bytes=VMEM_LIMIT),
        name="ssm_out",
    )(u8, xf, xb, wy)


def _ssm(us, tables, chained, mc):
    bq, wy, a8re, a8im, alre, alim = tables
    ncb, rows_tb, _ = us.shape
    sseg = rows_tb // NB
    mtot = sseg // SCAN_BLOCK
    nq = bq.shape[1]
    e0 = jnp.zeros((nq, SUBLANES, 2 * LANES), F32)
    src = us.reshape(ncb, mtot, SCAN_BLOCK, NB, LANES)
    fwd = lambda s, e, relayout, chain: _scan(
        s, e, bq[0], a8re[0], a8im[0], alre[0], alim[0], mc, False, relayout, chain)
    bwd = lambda s, e, chain: _scan(
        s, e, bq[1], a8re[1], a8im[1], alre[1], alim[1], mc, True, False, chain)
    u8, xf, ef = fwd(src, e0, True, False)
    xb, eb = bwd(u8, e0, False)
    if chained:
        xf, _ = fwd(u8, ef, False, True)
        xb, _ = bwd(u8, eb, True)
    y = _ssm_out(u8, xf, xb, wy, ncb, mc)
    return y.reshape(ncb, rows_tb, LANES)


def _ffn_chunks(hidden, step=1024):
    edges = list(range(0, hidden, step)) + [hidden]
    return list(zip(edges[:-1], edges[1:]))


def _mix_ffn_kernel(x_ref, f_ref, y_ref, sa_ref, sb_ref, gpost_ref, gpre_ref, gfpost_ref,
                    wfo_ref, wval_ref, wgate_ref, wout_ref, wfg_ref, wfu_ref, wfd_ref, o_ref,
                    *, tt, d, fw, sw, hidden):
    rows = NB * tt
    ys = jnp.concatenate(
        [jnp.concatenate(
            [y_ref[cb, pl.ds(b, tt, stride=NB), :] for b in range(NB)], axis=0)
         for cb in range(sw // LANES)], axis=1)
    z = jax.nn.gelu(ys).astype(BF16)
    f = jnp.concatenate(
        [f_ref[g].reshape(rows, LANES) for g in range(fw // LANES)], axis=1).astype(BF16)
    br_a = _dot(f, wfo_ref[...])
    br_b = _dot(z, wval_ref[...]) * jax.nn.sigmoid(_dot(z, wgate_ref[...]))
    sa = sa_ref[...].reshape(rows, d).astype(F32)
    sb = sb_ref[...].reshape(rows, d).astype(F32)
    merged = (sa * br_a + sb * br_b).astype(BF16)
    m = _dot(merged, wout_ref[...])
    x1 = x_ref[...].reshape(rows, d) + _rms(m, gpost_ref[...])
    h2 = _rms(x1, gpre_ref[...]).astype(BF16)
    acc = None
    for c0, c1 in _ffn_chunks(hidden):
        g = _dot(h2, wfg_ref[:, c0:c1])
        u = _dot(h2, wfu_ref[:, c0:c1])
        part = _dot((jax.nn.silu(g) * u).astype(BF16), wfd_ref[c0:c1, :])
        acc = part if acc is None else acc + part
    o_ref[...] = (x1 + _rms(acc, gfpost_ref[...])).reshape(NB, tt, d)


def _mix_ffn(xv, f, y, sa, sb, gpost, gpre, gfpost, wfo, wval, wgate, wout, wfg, wfu, wfd, tt):
    nb, sseg, d = xv.shape
    fw = f.shape[0] * LANES
    nblk = y.shape[0]
    sw = nblk * LANES
    hidden = wfg.shape[-1]
    n = sseg // tt
    kern = functools.partial(_mix_ffn_kernel, tt=tt, d=d, fw=fw, sw=sw, hidden=hidden)
    tok = lambda width: pl.BlockSpec((NB, tt, width), lambda i: (0, i, 0))
    tmaj = pl.BlockSpec((nblk, NB * tt, LANES), lambda i: (0, i, 0))
    return pl.pallas_call(
        kern,
        grid=(n,),
        in_specs=[tok(d), pl.BlockSpec((fw // LANES, NB, tt, LANES), lambda i: (0, 0, i, 0)),
                  tmaj, tok(d), tok(d),
                  _const_spec((1, d)), _const_spec((1, d)), _const_spec((1, d)),
                  _const_spec(wfo.shape), _const_spec(wval.shape), _const_spec(wgate.shape),
                  _const_spec(wout.shape), _const_spec(wfg.shape), _const_spec(wfu.shape),
                  _const_spec(wfd.shape)],
        out_specs=tok(d),
        out_shape=jax.ShapeDtypeStruct((NB, sseg, d), F32),
        compiler_params=pltpu.CompilerParams(
            dimension_semantics=("arbitrary",), vmem_limit_bytes=VMEM_LIMIT),
        name="mix_ffn",
    )(xv, f, y, sa, sb, gpost, gpre, gfpost, wfo, wval, wgate, wout, wfg, wfu, wfd)


def _pick(total, want):
    t = min(want, total)
    assert total % t == 0, (total, t)
    return t


def _encoder_layer(x, p):
    b, s, d = x.shape
    assert b in (1, NB), "one sequence (split in NB segments) or NB sequences"
    sseg = (b * s) // NB
    fw = p["w_fnet_out"].shape[0]
    sw = p["w_glu_val"].shape[0]
    xv = x.reshape(NB, sseg, d)
    tt = _pick(sseg, 64)

    uf, us, sa, sb = _in_proj(xv, p["norm_mix_pre"], p["w_in"], fw, sw, _pick(sseg, 128))
    ng = fw // LANES
    f = _fourier_mix(uf.reshape(ng, b, s, LANES)).reshape(ng, NB, sseg, LANES)

    tables = _ssm_tables(p["lam_re"], p["lam_im"], p["log_dt"], p["b_re"], p["b_im"],
                         p["c_re"], p["c_im"], p["d_skip"], sseg)
    ys = _ssm(us, tables, chained=(b == 1), mc=_pick(sseg // SCAN_BLOCK, 32))

    y = _mix_ffn(xv, f, ys, sa, sb, p["norm_mix_post"], p["norm_ffn_pre"], p["norm_ffn_post"],
                 p["w_fnet_out"], p["w_glu_val"], p["w_glu_gate"], p["w_out"],
                 p["w_ffn_gate"], p["w_ffn_up"], p["w_ffn_down"], tt)
    return y.reshape(b, s, d)


def _layer_params(l, norm_mix_pre, norm_mix_post, norm_ffn_pre, norm_ffn_post, w_in, w_fnet_out,
                  lam_re, lam_im, log_dt, b_re, b_im, c_re, c_im, d_skip, w_glu_val, w_glu_gate,
                  w_out, w_ffn_gate, w_ffn_up, w_ffn_down):
    row = lambda v: v[l].astype(F32).reshape(1, -1)
    w = lambda v: v[l].astype(BF16)
    return dict(
        norm_mix_pre=row(norm_mix_pre), norm_mix_post=row(norm_mix_post),
        norm_ffn_pre=row(norm_ffn_pre), norm_ffn_post=row(norm_ffn_post),
        w_in=w(w_in), w_fnet_out=w(w_fnet_out),
        lam_re=lam_re[l], lam_im=lam_im[l], log_dt=log_dt[l],
        b_re=b_re[l], b_im=b_im[l], c_re=c_re[l], c_im=c_im[l], d_skip=row(d_skip),
        w_glu_val=w(w_glu_val), w_glu_gate=w(w_glu_gate), w_out=w(w_out),
        w_ffn_gate=w(w_ffn_gate), w_ffn_up=w(w_ffn_up), w_ffn_down=w(w_ffn_down))


def kernel(x_prompt, x_sample, norm_mix_pre, norm_mix_post, norm_ffn_pre, norm_ffn_post, w_in, w_fnet_out, lam_re, lam_im, log_dt, b_re, b_im, c_re, c_im, d_skip, w_glu_val, w_glu_gate, w_out, w_ffn_gate, w_ffn_up, w_ffn_down):
    weights = (norm_mix_pre, norm_mix_post, norm_ffn_pre, norm_ffn_post, w_in, w_fnet_out,
               lam_re, lam_im, log_dt, b_re, b_im, c_re, c_im, d_skip, w_glu_val, w_glu_gate,
               w_out, w_ffn_gate, w_ffn_up, w_ffn_down)
    depth = w_in.shape[0]
    outs = []
    for x in (x_prompt, x_sample):
        for l in range(depth):
            x = _encoder_layer(x, _layer_params(l, *weights))
        outs.append(x)
    return tuple(outs)
```

```python
import functools
import math

import numpy as np
import jax
import jax.numpy as jnp
from jax import lax
from jax.experimental import pallas as pl
from jax.experimental.pallas import tpu as pltpu

F32 = jnp.float32
BF16 = jnp.bfloat16

EPS = 1e-6
SUBLANES = 8
LANES = 128
NB = SUBLANES

FNET_GROUP_DIM = 128
SSM_GROUP_DIM = 16
GROUPS_PER_BLOCK = LANES // SSM_GROUP_DIM
SCAN_BLOCK = LANES // SSM_GROUP_DIM
PAIR = 2
SCAN_INTERLEAVE = 4

FFT_N2 = 128
FFT_S2_BLOCK = 16
FFT_K1_BLOCK = 8
FFT_VMEM_BUDGET = 40 * 1024 * 1024
FFT_GROUPS_PER_PASS = 2
FFT_ROW_PAD = SUBLANES
VMEM_LIMIT = 56 * 1024 * 1024


def _dot(a, b):
    return jnp.dot(a, b, preferred_element_type=F32)


def _rms(x, g):
    return x * lax.rsqrt(jnp.mean(x * x, axis=-1, keepdims=True) + EPS) * g


def _const_spec(shape):
    nd = len(shape)
    return pl.BlockSpec(shape, lambda *_: (0,) * nd, pipeline_mode=pl.Buffered(1))


def _transpose_pieces(vs):
    n = GROUPS_PER_BLOCK
    rows = vs[0].shape[0]
    piece = lax.broadcasted_iota(jnp.int32, (rows, LANES), 1) // SSM_GROUP_DIM
    cur = list(vs)
    d = 1
    while d < n:
        upper = (piece & d) != 0
        nxt = [None] * n
        for c in range(n):
            if c & d == 0:
                lo, hi = cur[c], cur[c | d]
                nxt[c] = jnp.where(upper, pltpu.roll(hi, d * SSM_GROUP_DIM, 1), lo)
                nxt[c | d] = jnp.where(upper, hi, pltpu.roll(lo, LANES - d * SSM_GROUP_DIM, 1))
        cur = nxt
        d *= 2
    return cur


def _in_proj_kernel(x_ref, g_ref, w_ref, uf_ref, u8_ref, sa_ref, sb_ref, us_scr, *, tt, fw, sw, d):
    rows = NB * tt
    x = x_ref[...].reshape(rows, d)
    h = _rms(x, g_ref[...]).astype(BF16)
    uf = _dot(h, w_ref[:, 0:fw])
    for g in range(fw // LANES):
        uf_ref[g] = uf[:, g * LANES:(g + 1) * LANES].reshape(NB, tt, LANES)
    us = _dot(h, w_ref[:, fw:fw + sw])
    for cb in range(sw // LANES):
        for b in range(NB):
            us_scr[cb, pl.ds(b, tt, stride=NB), :] = us[b * tt:(b + 1) * tt, cb * LANES:(cb + 1) * LANES]
    for cb in range(sw // LANES):
        vs = [jnp.concatenate([us_scr[cb, pl.ds((SCAN_BLOCK * m + j) * NB, NB), :]
                               for m in range(tt // SCAN_BLOCK)], axis=0)
              for j in range(SCAN_BLOCK)]
        folded = _transpose_pieces(vs)
        for gl in range(GROUPS_PER_BLOCK):
            g = cb * GROUPS_PER_BLOCK + gl
            u8_ref[g // PAIR, :, (g % PAIR) * LANES:(g % PAIR + 1) * LANES] = folded[gl].astype(BF16)
    o = fw + sw
    sa_ref[...] = jax.nn.sigmoid(_dot(h, w_ref[:, o:o + d])).reshape(NB, tt, d).astype(BF16)
    sb_ref[...] = jax.nn.sigmoid(_dot(h, w_ref[:, o + d:o + 2 * d])).reshape(NB, tt, d).astype(BF16)


def _in_proj(xv, gain, w_in, fw, sw, tt):
    nb, sseg, d = xv.shape
    n = sseg // tt
    nq = sw // (PAIR * SSM_GROUP_DIM)
    assert tt % SCAN_BLOCK == 0
    kern = functools.partial(_in_proj_kernel, tt=tt, fw=fw, sw=sw, d=d)
    return pl.pallas_call(
        kern,
        grid=(n,),
        in_specs=[
            pl.BlockSpec((NB, tt, d), lambda i: (0, i, 0)),
            _const_spec((1, d)),
            _const_spec(w_in.shape),
        ],
        out_specs=[
            pl.BlockSpec((fw // LANES, NB, tt, LANES), lambda i: (0, 0, i, 0)),
            pl.BlockSpec((nq, tt, PAIR * LANES), lambda i: (0, i, 0)),
            pl.BlockSpec((NB, tt, d), lambda i: (0, i, 0)),
            pl.BlockSpec((NB, tt, d), lambda i: (0, i, 0)),
        ],
        out_shape=[
            jax.ShapeDtypeStruct((fw // LANES, NB, sseg, LANES), F32),
            jax.ShapeDtypeStruct((nq, sseg, PAIR * LANES), BF16),
            jax.ShapeDtypeStruct((NB, sseg, d), BF16),
            jax.ShapeDtypeStruct((NB, sseg, d), BF16),
        ],
        scratch_shapes=[pltpu.VMEM((sw // LANES, NB * tt, LANES), F32)],
        compiler_params=pltpu.CompilerParams(
            dimension_semantics=("arbitrary",), vmem_limit_bytes=VMEM_LIMIT),
        name="in_proj",
    )(xv, gain, w_in)


def _fft_tables(s):
    n2 = FFT_N2
    n1 = s // n2
    s1 = np.arange(n1, dtype=np.int64)
    k1 = np.arange(n1, dtype=np.int64)
    s2 = np.arange(n2, dtype=np.int64)
    idx = (k1[None, :, None] * s1[None, None, :] * n2 + s2[:, None, None] * k1[None, :, None]) % s
    ang = 2.0 * np.pi * idx.astype(np.float64) / s
    g = np.stack([np.cos(ang), -np.sin(ang)], axis=1) / math.sqrt(n1)
    g = g.reshape(n2, 2 * n1, n1)
    k2 = np.arange(n2, dtype=np.int64)
    ang2 = 2.0 * np.pi * ((k2[:, None] * s2[None, :]) % n2).astype(np.float64) / n2
    c2, s2m = np.cos(ang2), np.sin(ang2)
    h = np.block([[c2, s2m], [-s2m, c2]]) / math.sqrt(n2)
    c = np.arange(FNET_GROUP_DIM, dtype=np.int64)
    angc = 2.0 * np.pi * ((c[:, None] * c[None, :]) % FNET_GROUP_DIM).astype(np.float64) / FNET_GROUP_DIM
    cs = np.concatenate([np.cos(angc), np.sin(angc)], axis=0) / math.sqrt(FNET_GROUP_DIM)
    return g.astype(np.float32), h.astype(np.float32), cs.astype(np.float32)


def _fft_kernel(x_ref, g_ref, h_ref, cs_ref, o_ref, xs_scr, a_scr, os_scr,
                *, n1, n2, ng, p1, s2b, k1b):
    p = pl.program_id(2)
    kstride = 2 * n2 + FFT_ROW_PAD

    @pl.when(p < p1)
    def _stage1():
        xs_scr[...] = x_ref[...].reshape(ng, n1 * s2b, LANES)
        for j in range(s2b):
            xj = jnp.concatenate(
                [xs_scr[g, pl.ds(j, n1, stride=s2b), :] for g in range(ng)], axis=1).astype(BF16)
            r = _dot(g_ref[j], xj)
            s2 = p * s2b + j
            for g in range(ng):
                a_scr[g, pl.ds(s2, n1, stride=kstride), :] = r[0:n1, g * LANES:(g + 1) * LANES]
                a_scr[g, pl.ds(n2 + s2, n1, stride=kstride), :] = r[n1:2 * n1, g * LANES:(g + 1) * LANES]

    @pl.when(p >= p1)
    def _stage2():
        for j in range(k1b):
            row0 = pl.multiple_of(((p - p1) * k1b + j) * kstride, SUBLANES)
            a = jnp.concatenate(
                [a_scr[g, pl.ds(row0, 2 * n2), :] for g in range(ng)], axis=1).astype(BF16)
            y = _dot(h_ref[...], a).astype(BF16)
            for g in range(ng):
                yg = jnp.concatenate(
                    [y[0:n2, g * LANES:(g + 1) * LANES], y[n2:2 * n2, g * LANES:(g + 1) * LANES]], axis=1)
                os_scr[g, pl.ds(j, n2, stride=k1b), :] = _dot(yg, cs_ref[...])
        o_ref[...] = os_scr[...].reshape(ng, n2, k1b, LANES)


def _fourier_mix(u):
    ngroups, b, s, c = u.shape
    assert c == FNET_GROUP_DIM == LANES
    n2 = FFT_N2
    n1 = s // n2
    ng = FFT_GROUPS_PER_PASS
    kstride = 2 * n2 + FFT_ROW_PAD

    def vmem_bytes(s2b, k1b):
        f32 = 4 * ng * c
        return (f32 * n1 * kstride + 3 * f32 * n1 * s2b + 3 * f32 * n2 * k1b
                + 2 * 2 * s2b * 2 * n1 * max(n1, LANES))

    s2b, k1b = FFT_S2_BLOCK, FFT_K1_BLOCK
    if vmem_bytes(2 * s2b, 2 * k1b) <= FFT_VMEM_BUDGET and n2 % (2 * s2b) == 0 and n1 % (2 * k1b) == 0:
        s2b, k1b = 2 * s2b, 2 * k1b
    assert n1 * n2 == s and n1 % k1b == 0 and n2 % s2b == 0 and ngroups % ng == 0
    p1, p2 = n2 // s2b, n1 // k1b
    g_np, h_np, cs_np = _fft_tables(s)
    g_t = jnp.asarray(g_np).astype(BF16)
    h_t = jnp.asarray(h_np).astype(BF16)
    cs_t = jnp.asarray(cs_np).astype(BF16)

    f = pl.pallas_call(
        functools.partial(_fft_kernel, n1=n1, n2=n2, ng=ng, p1=p1, s2b=s2b, k1b=k1b),
        grid=(b, ngroups // ng, p1 + p2),
        in_specs=[
            pl.BlockSpec((ng, None, n1, s2b, c),
                         lambda bi, gi, p: (gi, bi, 0, jnp.minimum(p, p1 - 1), 0)),
            pl.BlockSpec((s2b, 2 * n1, n1), lambda bi, gi, p: (jnp.minimum(p, p1 - 1), 0, 0)),
            _const_spec((2 * n2, 2 * n2)),
            _const_spec((2 * FNET_GROUP_DIM, FNET_GROUP_DIM)),
        ],
        out_specs=pl.BlockSpec((ng, None, n2, k1b, c),
                               lambda bi, gi, p: (gi, bi, 0, jnp.maximum(p - p1, 0), 0)),
        out_shape=jax.ShapeDtypeStruct((ngroups, b, n2, n1, c), F32),
        scratch_shapes=[pltpu.VMEM((ng, n1 * s2b, c), F32),
                        pltpu.VMEM((ng, n1 * kstride, c), F32),
                        pltpu.VMEM((ng, n2 * k1b, c), F32)],
        compiler_params=pltpu.CompilerParams(
            dimension_semantics=("arbitrary", "arbitrary", "arbitrary"),
            vmem_limit_bytes=VMEM_LIMIT),
        name="fft",
    )(u.reshape(ngroups, b, n1, n2, c), g_t, h_t, cs_t)
    return f.reshape(ngroups, b, s, c)


def _ssm_tables(lam_re, lam_im, log_dt, b_re, b_im, c_re, c_im, d_skip, seg_len):
    assert PAIR == 2
    ndir, ng, p = lam_re.shape
    hd = b_re.shape[-1]
    r = SCAN_BLOCK
    nq = ng // PAIR
    lam_re = lam_re.astype(F32)
    lam_im = lam_im.astype(F32)
    dt = jnp.exp(log_dt.astype(F32))[..., None]

    def apow(k):
        mag = jnp.exp(lam_re * dt * k)
        return mag * jnp.cos(lam_im * dt * k), mag * jnp.sin(lam_im * dt * k)

    ar, ai = apow(1.0)
    den = lam_re * lam_re + lam_im * lam_im
    qr = ((ar - 1.0) * lam_re + ai * lam_im) / den
    qi = (ai * lam_re - (ar - 1.0) * lam_im) / den
    bre = qr[..., None] * b_re.astype(F32) - qi[..., None] * b_im.astype(F32)
    bim = qr[..., None] * b_im.astype(F32) + qi[..., None] * b_re.astype(F32)
    cr = c_re.astype(F32)
    ci = c_im.astype(F32)

    ks = jnp.arange(r + 1, dtype=F32).reshape(r + 1, 1, 1, 1)
    pr, pi = apow(ks)
    abr = pr[:r, ..., None] * bre - pi[:r, ..., None] * bim
    abi = pr[:r, ..., None] * bim + pi[:r, ..., None] * bre

    def in_rows(x):
        return jnp.transpose(x, (1, 0, 3, 2)).reshape(ng, r * hd, p)

    bp_re = jnp.stack([in_rows(abr[::-1, 0]), in_rows(abr[:, 1])])
    bp_im = jnp.stack([in_rows(abi[::-1, 0]), in_rows(abi[:, 1])])

    klag = jnp.sum(cr[None, :, :, :, :, None] * abr[:, :, :, None, :, :]
                   - ci[None, :, :, :, :, None] * abi[:, :, :, None, :, :], axis=4)
    kzero = jnp.zeros_like(klag[0, 0])
    kin = jnp.stack([
        jnp.stack([(klag[i - j, 0] if i >= j else kzero) + (klag[j - i, 1] if j >= i else kzero)
                   for i in range(r)]) for j in range(r)])
    kin = jnp.transpose(kin, (2, 0, 4, 1, 3)).reshape(ng, r * hd, r * hd)
    dsk = d_skip.astype(F32).reshape(ng, hd)
    kin = kin + jnp.eye(r * hd, dtype=F32)[None] * jnp.tile(dsk, (1, r))[:, None, :]

    def state_out(d, pows):
        er = cr[d][None] * pr[pows, d][:, :, None, :] - ci[d][None] * pi[pows, d][:, :, None, :]
        ei = cr[d][None] * pi[pows, d][:, :, None, :] + ci[d][None] * pr[pows, d][:, :, None, :]
        to_rows = lambda e: jnp.transpose(e, (1, 3, 0, 2)).reshape(ng, p, r * hd)
        return to_rows(er), -to_rows(ei)

    cf_re, cf_im = state_out(0, jnp.arange(1, r + 1))
    cb_re, cb_im = state_out(1, r - jnp.arange(r))

    def pair_bd(m):
        lead = m.shape[:-3]
        rr, cc = m.shape[-2:]
        m = m.reshape(lead + (nq, PAIR, rr, cc))
        m0, m1 = m[..., 0, :, :], m[..., 1, :, :]
        z = jnp.zeros_like(m0)
        return jnp.concatenate([jnp.concatenate([m0, z], axis=-1),
                                jnp.concatenate([z, m1], axis=-1)], axis=-2)

    bq = jnp.concatenate([pair_bd(bp_re), pair_bd(bp_im)], axis=-1).astype(BF16)
    wy = jnp.concatenate([pair_bd(kin), pair_bd(cf_re), pair_bd(cf_im),
                          pair_bd(cb_re), pair_bd(cb_im)], axis=-2).astype(BF16)

    def lanes(v):
        v = v.reshape(ndir, nq, 1, PAIR * p)
        return jnp.broadcast_to(v, (ndir, nq, SUBLANES, PAIR * p))

    a8r, a8i = apow(float(r))
    alr, ali = apow(float(seg_len))
    return bq, wy, lanes(a8r), lanes(a8i), lanes(alr), lanes(ali)


def _scan_kernel(u8_ref, e_ref, bq_ref, a8re_ref, a8im_ref, alre_ref, alim_ref,
                 x_ref, eo_ref, d_scr, st_scr, *, mc, nq, reverse, chain):
    i = pl.program_id(0)
    n = pl.num_programs(0)
    rows = mc * SUBLANES
    hl = LANES

    @pl.when(i == 0)
    def _init():
        if not chain:
            st_scr[...] = jnp.zeros(st_scr.shape, F32)
        else:
            row = lax.broadcasted_iota(jnp.int32, (SUBLANES, hl), 0)
            shift = SUBLANES - 1 if reverse else 1
            keep = (row <= SUBLANES - 2) if reverse else (row >= 1)
            for q in range(nq):
                er = pltpu.roll(e_ref[q, :, 0:hl], shift, 0)
                ei = pltpu.roll(e_ref[q, :, hl:2 * hl], shift, 0)
                lr = alre_ref[q]
                li = alim_ref[q]
                xr = jnp.zeros((SUBLANES, hl), F32)
                xi = jnp.zeros((SUBLANES, hl), F32)
                for _ in range(SUBLANES - 1):
                    pr = pltpu.roll(xr, shift, 0)
                    pi = pltpu.roll(xi, shift, 0)
                    xr = jnp.where(keep, lr * pr - li * pi + er, 0.0)
                    xi = jnp.where(keep, lr * pi + li * pr + ei, 0.0)
                st_scr[q, :, 0:hl] = xr
                st_scr[q, :, hl:2 * hl] = xi

    for q0 in range(0, nq, SCAN_INTERLEAVE):
        qs = list(range(q0, min(q0 + SCAN_INTERLEAVE, nq)))
        for k, q in enumerate(qs):
            d_scr[k] = _dot(u8_ref[q], bq_ref[q])
        a_re = [a8re_ref[q] for q in qs]
        a_im = [a8im_ref[q] for q in qs]

        def body(step, carry, a_re=a_re, a_im=a_im, nk=len(qs)):
            m = (mc - 1 - step) if reverse else step
            r = pl.multiple_of(m * SUBLANES, SUBLANES)
            out = []
            for k in range(nk):
                re, im = carry[2 * k], carry[2 * k + 1]
                d_re = d_scr[k, pl.ds(r, SUBLANES), 0:hl]
                d_im = d_scr[k, pl.ds(r, SUBLANES), hl:2 * hl]
                d_scr[k, pl.ds(r, SUBLANES), 0:hl] = re
                d_scr[k, pl.ds(r, SUBLANES), hl:2 * hl] = im
                out.append(a_re[k] * re - a_im[k] * im + d_re)
                out.append(a_re[k] * im + a_im[k] * re + d_im)
            return tuple(out)

        init = []
        for q in qs:
            init += [st_scr[q, :, 0:hl], st_scr[q, :, hl:2 * hl]]
        fin = lax.fori_loop(0, mc, body, tuple(init), unroll=2)
        for k, q in enumerate(qs):
            st_scr[q, :, 0:hl] = fin[2 * k]
            st_scr[q, :, hl:2 * hl] = fin[2 * k + 1]
            x_ref[q] = d_scr[k].astype(BF16)

    @pl.when(i == n - 1)
    def _fin():
        eo_ref[...] = st_scr[...]


def _scan(u8, e_in, bq, a8re, a8im, alre, alim, mc, reverse, chain):
    nq = bq.shape[0]
    mtot = u8.shape[1] // NB
    n = mtot // mc
    rows = mc * NB
    order = (lambda i: n - 1 - i) if reverse else (lambda i: i)
    st_shape = (nq, SUBLANES, 2 * LANES)
    row_spec = pl.BlockSpec((nq, rows, 2 * LANES), lambda i: (0, order(i), 0))
    kern = functools.partial(_scan_kernel, mc=mc, nq=nq, reverse=reverse, chain=chain)
    return pl.pallas_call(
        kern,
        grid=(n,),
        in_specs=[row_spec, _const_spec(st_shape), _const_spec(bq.shape),
                  _const_spec(a8re.shape), _const_spec(a8im.shape),
                  _const_spec(alre.shape), _const_spec(alim.shape)],
        out_specs=[row_spec, _const_spec(st_shape)],
        out_shape=[jax.ShapeDtypeStruct((nq, mtot * NB, 2 * LANES), BF16),
                   jax.ShapeDtypeStruct(st_shape, F32)],
        scratch_shapes=[pltpu.VMEM((SCAN_INTERLEAVE, rows, 2 * LANES), F32),
                        pltpu.VMEM(st_shape, F32)],
        compiler_params=pltpu.CompilerParams(
            dimension_semantics=("arbitrary",), vmem_limit_bytes=VMEM_LIMIT),
        name="ssm_scan_" + ("bwd" if reverse else "fwd") + ("_chain" if chain else ""),
    )(u8, e_in, bq, a8re, a8im, alre, alim)


def _ssm_out_kernel(u8_ref, xf_ref, xb_ref, wy_ref, y8_ref, *, nq):
    for q in range(nq):
        lhs = jnp.concatenate([u8_ref[q], xf_ref[q], xb_ref[q]], axis=1)
        y8_ref[q] = _dot(lhs, wy_ref[q])


def _ssm_out(u8, xf, xb, wy, mc):
    nq, rows_total, _ = u8.shape
    n = rows_total // (mc * NB)
    row_spec = pl.BlockSpec((nq, mc * NB, 2 * LANES), lambda i: (0, i, 0))
    return pl.pallas_call(
        functools.partial(_ssm_out_kernel, nq=nq),
        grid=(n,),
        in_specs=[row_spec, row_spec, row_spec, _const_spec(wy.shape)],
        out_specs=row_spec,
        out_shape=jax.ShapeDtypeStruct((nq, rows_total, 2 * LANES), F32),
        compiler_params=pltpu.CompilerParams(
            dimension_semantics=("arbitrary",), vmem_limit_bytes=VMEM_LIMIT),
        name="ssm_out",
    )(u8, xf, xb, wy)


def _ssm(u8, tables, chained, mc):
    bq, wy, a8re, a8im, alre, alim = tables
    nq = bq.shape[1]
    e0 = jnp.zeros((nq, SUBLANES, 2 * LANES), F32)
    fwd = lambda e, chain: _scan(u8, e, bq[0], a8re[0], a8im[0], alre[0], alim[0], mc, False, chain)
    bwd = lambda e, chain: _scan(u8, e, bq[1], a8re[1], a8im[1], alre[1], alim[1], mc, True, chain)
    xf, ef = fwd(e0, False)
    xb, eb = bwd(e0, False)
    if chained:
        xf, _ = fwd(ef, True)
        xb, _ = bwd(eb, True)
    return _ssm_out(u8, xf, xb, wy, mc)


def _ffn_chunks(hidden, step=1024):
    edges = list(range(0, hidden, step)) + [hidden]
    return list(zip(edges[:-1], edges[1:]))


def _mix_ffn_kernel(x_ref, f_ref, y8_ref, sa_ref, sb_ref, gpost_ref, gpre_ref, gfpost_ref,
                    wfo_ref, wval_ref, wgate_ref, wout_ref, wfg_ref, wfu_ref, wfd_ref, o_ref, y_scr,
                    *, tt, d, fw, sw, hidden):
    rows = NB * tt
    qpb = GROUPS_PER_BLOCK // PAIR
    for cb in range(sw // LANES):
        vs = [y8_ref[cb * qpb + gl // PAIR, :, (gl % PAIR) * LANES:(gl % PAIR + 1) * LANES]
              for gl in range(GROUPS_PER_BLOCK)]
        unfolded = _transpose_pieces(vs)
        for i in range(SCAN_BLOCK):
            for m in range(tt // SCAN_BLOCK):
                y_scr[cb, pl.ds((SCAN_BLOCK * m + i) * NB, NB), :] = unfolded[i][m * NB:(m + 1) * NB, :]
    ys = jnp.concatenate(
        [jnp.concatenate(
            [y_scr[cb, pl.ds(b, tt, stride=NB), :] for b in range(NB)], axis=0)
         for cb in range(sw // LANES)], axis=1)
    z = jax.nn.gelu(ys).astype(BF16)
    f = jnp.concatenate(
        [f_ref[g].reshape(rows, LANES) for g in range(fw // LANES)], axis=1).astype(BF16)
    br_a = _dot(f, wfo_ref[...])
    br_b = _dot(z, wval_ref[...]) * jax.nn.sigmoid(_dot(z, wgate_ref[...]))
    sa = sa_ref[...].reshape(rows, d).astype(F32)
    sb = sb_ref[...].reshape(rows, d).astype(F32)
    merged = (sa * br_a + sb * br_b).astype(BF16)
    m = _dot(merged, wout_ref[...])
    x1 = x_ref[...].reshape(rows, d) + _rms(m, gpost_ref[...])
    h2 = _rms(x1, gpre_ref[...]).astype(BF16)
    acc = None
    for c0, c1 in _ffn_chunks(hidden):
        g = _dot(h2, wfg_ref[:, c0:c1])
        u = _dot(h2, wfu_ref[:, c0:c1])
        part = _dot((jax.nn.silu(g) * u).astype(BF16), wfd_ref[c0:c1, :])
        acc = part if acc is None else acc + part
    o_ref[...] = (x1 + _rms(acc, gfpost_ref[...])).reshape(NB, tt, d)


def _mix_ffn(xv, f, y8, sa, sb, gpost, gpre, gfpost, wfo, wval, wgate, wout, wfg, wfu, wfd, tt):
    nb, sseg, d = xv.shape
    fw = f.shape[0] * LANES
    nq = y8.shape[0]
    sw = nq * PAIR * SSM_GROUP_DIM
    hidden = wfg.shape[-1]
    n = sseg // tt
    assert tt % SCAN_BLOCK == 0
    kern = functools.partial(_mix_ffn_kernel, tt=tt, d=d, fw=fw, sw=sw, hidden=hidden)
    tok = lambda width: pl.BlockSpec((NB, tt, width), lambda i: (0, i, 0))
    tmaj = pl.BlockSpec((nq, tt, PAIR * LANES), lambda i: (0, i, 0))
    return pl.pallas_call(
        kern,
        grid=(n,),
        in_specs=[tok(d), pl.BlockSpec((fw // LANES, NB, tt, LANES), lambda i: (0, 0, i, 0)),
                  tmaj, tok(d), tok(d),
                  _const_spec((1, d)), _const_spec((1, d)), _const_spec((1, d)),
                  _const_spec(wfo.shape), _const_spec(wval.shape), _const_spec(wgate.shape),
                  _const_spec(wout.shape), _const_spec(wfg.shape), _const_spec(wfu.shape),
                  _const_spec(wfd.shape)],
        out_specs=tok(d),
        out_shape=jax.ShapeDtypeStruct((NB, sseg, d), F32),
        scratch_shapes=[pltpu.VMEM((sw // LANES, NB * tt, LANES), F32)],
        compiler_params=pltpu.CompilerParams(
            dimension_semantics=("arbitrary",), vmem_limit_bytes=VMEM_LIMIT),
        name="mix_ffn",
    )(xv, f, y8, sa, sb, gpost, gpre, gfpost, wfo, wval, wgate, wout, wfg, wfu, wfd)


def _pick(total, want):
    t = min(want, total)
    assert total % t == 0, (total, t)
    return t


def _encoder_layer(x, p):
    b, s, d = x.shape
    assert b in (1, NB), "one sequence (split in NB segments) or NB sequences"
    sseg = (b * s) // NB
    fw = p["w_fnet_out"].shape[0]
    sw = p["w_glu_val"].shape[0]
    xv = x.reshape(NB, sseg, d)
    tt = _pick(sseg, 64)

    uf, u8, sa, sb = _in_proj(xv, p["norm_mix_pre"], p["w_in"], fw, sw, _pick(sseg, 128))
    ng = fw // LANES
    f = _fourier_mix(uf.reshape(ng, b, s, LANES)).reshape(ng, NB, sseg, LANES)

    tables = _ssm_tables(p["lam_re"], p["lam_im"], p["log_dt"], p["b_re"], p["b_im"],
                         p["c_re"], p["c_im"], p["d_skip"], sseg)
    ys = _ssm(u8, tables, chained=(b == 1), mc=_pick(sseg // SCAN_BLOCK, 32))

    y = _mix_ffn(xv, f, ys, sa, sb, p["norm_mix_post"], p["norm_ffn_pre"], p["norm_ffn_post"],
                 p["w_fnet_out"], p["w_glu_val"], p["w_glu_gate"], p["w_out"],
                 p["w_ffn_gate"], p["w_ffn_up"], p["w_ffn_down"], tt)
    return y.reshape(b, s, d)


def _layer_params(l, norm_mix_pre, norm_mix_post, norm_ffn_pre, norm_ffn_post, w_in, w_fnet_out,
                  lam_re, lam_im, log_dt, b_re, b_im, c_re, c_im, d_skip, w_glu_val, w_glu_gate,
                  w_out, w_ffn_gate, w_ffn_up, w_ffn_down):
    row = lambda v: v[l].astype(F32).reshape(1, -1)
    w = lambda v: v[l].astype(BF16)
    return dict(
        norm_mix_pre=row(norm_mix_pre), norm_mix_post=row(norm_mix_post),
        norm_ffn_pre=row(norm_ffn_pre), norm_ffn_post=row(norm_ffn_post),
        w_in=w(w_in), w_fnet_out=w(w_fnet_out),
        lam_re=lam_re[l], lam_im=lam_im[l], log_dt=log_dt[l],
        b_re=b_re[l], b_im=b_im[l], c_re=c_re[l], c_im=c_im[l], d_skip=row(d_skip),
        w_glu_val=w(w_glu_val), w_glu_gate=w(w_glu_gate), w_out=w(w_out),
        w_ffn_gate=w(w_ffn_gate), w_ffn_up=w(w_ffn_up), w_ffn_down=w(w_ffn_down))


def kernel(x_prompt, x_sample, norm_mix_pre, norm_mix_post, norm_ffn_pre, norm_ffn_post, w_in, w_fnet_out, lam_re, lam_im, log_dt, b_re, b_im, c_re, c_im, d_skip, w_glu_val, w_glu_gate, w_out, w_ffn_gate, w_ffn_up, w_ffn_down):
    weights = (norm_mix_pre, norm_mix_post, norm_ffn_pre, norm_ffn_post, w_in, w_fnet_out,
               lam_re, lam_im, log_dt, b_re, b_im, c_re, c_im, d_skip, w_glu_val, w_glu_gate,
               w_out, w_ffn_gate, w_ffn_up, w_ffn_down)
    depth = w_in.shape[0]
    outs = []
    for x in (x_prompt, x_sample):
        for l in range(depth):
            x = _encoder_layer(x, _layer_params(l, *weights))
        outs.append(x)
    return tuple(outs)
```

```python
import functools
import math

import numpy as np
import jax
import jax.numpy as jnp
from jax import lax
from jax.experimental import pallas as pl
from jax.experimental.pallas import tpu as pltpu

F32 = jnp.float32
BF16 = jnp.bfloat16

EPS = 1e-6
SUBLANES = 8
LANES = 128
NB = SUBLANES

FNET_GROUP_DIM = 128
SSM_GROUP_DIM = 16
GROUPS_PER_BLOCK = LANES // SSM_GROUP_DIM
SCAN_BLOCK = LANES // SSM_GROUP_DIM
PAIR = 2
SCAN_INTERLEAVE = 4

TOKEN_TILE_T = 128

FFT_N2 = 128
FFT_S2_BLOCK = 16
FFT_K1_BLOCK = 8
FFT_VMEM_BUDGET = 40 * 1024 * 1024
FFT_GROUPS_PER_PASS = 2
FFT_ROW_PAD = SUBLANES
VMEM_LIMIT = 56 * 1024 * 1024


def _dot(a, b):
    return jnp.dot(a, b, preferred_element_type=F32)


def _rms(x, g):
    return x * lax.rsqrt(jnp.mean(x * x, axis=-1, keepdims=True) + EPS) * g


def _const_spec(shape):
    nd = len(shape)
    return pl.BlockSpec(shape, lambda *_: (0,) * nd, pipeline_mode=pl.Buffered(1))


def _transpose_pieces(vs):
    n = GROUPS_PER_BLOCK
    rows = vs[0].shape[0]
    piece = lax.broadcasted_iota(jnp.int32, (rows, LANES), 1) // SSM_GROUP_DIM
    cur = list(vs)
    d = 1
    while d < n:
        upper = (piece & d) != 0
        nxt = [None] * n
        for c in range(n):
            if c & d == 0:
                lo, hi = cur[c], cur[c | d]
                nxt[c] = jnp.where(upper, pltpu.roll(hi, d * SSM_GROUP_DIM, 1), lo)
                nxt[c | d] = jnp.where(upper, hi, pltpu.roll(lo, LANES - d * SSM_GROUP_DIM, 1))
        cur = nxt
        d *= 2
    return cur


def _in_proj_kernel(x_ref, g_ref, w_ref, uf_ref, u8_ref, sa_ref, sb_ref, us_scr, *, tt, fw, sw, d):
    rows = NB * tt
    x = x_ref[...].reshape(rows, d)
    h = _rms(x, g_ref[...]).astype(BF16)
    uf = _dot(h, w_ref[:, 0:fw])
    for g in range(fw // LANES):
        uf_ref[g] = uf[:, g * LANES:(g + 1) * LANES].reshape(NB, tt, LANES)
    us = _dot(h, w_ref[:, fw:fw + sw])
    for cb in range(sw // LANES):
        for b in range(NB):
            us_scr[cb, pl.ds(b, tt, stride=NB), :] = us[b * tt:(b + 1) * tt, cb * LANES:(cb + 1) * LANES]
    for cb in range(sw // LANES):
        vs = [jnp.concatenate([us_scr[cb, pl.ds((SCAN_BLOCK * m + j) * NB, NB), :]
                               for m in range(tt // SCAN_BLOCK)], axis=0)
              for j in range(SCAN_BLOCK)]
        folded = _transpose_pieces(vs)
        for gl in range(GROUPS_PER_BLOCK):
            g = cb * GROUPS_PER_BLOCK + gl
            u8_ref[g // PAIR, :, (g % PAIR) * LANES:(g % PAIR + 1) * LANES] = folded[gl].astype(BF16)
    o = fw + sw
    sa_ref[...] = jax.nn.sigmoid(_dot(h, w_ref[:, o:o + d])).reshape(NB, tt, d).astype(BF16)
    sb_ref[...] = jax.nn.sigmoid(_dot(h, w_ref[:, o + d:o + 2 * d])).reshape(NB, tt, d).astype(BF16)


def _in_proj(xv, gain, w_in, fw, sw, tt):
    nb, sseg, d = xv.shape
    n = sseg // tt
    nq = sw // (PAIR * SSM_GROUP_DIM)
    assert tt % SCAN_BLOCK == 0
    kern = functools.partial(_in_proj_kernel, tt=tt, fw=fw, sw=sw, d=d)
    return pl.pallas_call(
        kern,
        grid=(n,),
        in_specs=[
            pl.BlockSpec((NB, tt, d), lambda i: (0, i, 0)),
            _const_spec((1, d)),
            _const_spec(w_in.shape),
        ],
        out_specs=[
            pl.BlockSpec((fw // LANES, NB, tt, LANES), lambda i: (0, 0, i, 0)),
            pl.BlockSpec((nq, tt, PAIR * LANES), lambda i: (0, i, 0)),
            pl.BlockSpec((NB, tt, d), lambda i: (0, i, 0)),
            pl.BlockSpec((NB, tt, d), lambda i: (0, i, 0)),
        ],
        out_shape=[
            jax.ShapeDtypeStruct((fw // LANES, NB, sseg, LANES), F32),
            jax.ShapeDtypeStruct((nq, sseg, PAIR * LANES), BF16),
            jax.ShapeDtypeStruct((NB, sseg, d), BF16),
            jax.ShapeDtypeStruct((NB, sseg, d), BF16),
        ],
        scratch_shapes=[pltpu.VMEM((sw // LANES, NB * tt, LANES), F32)],
        compiler_params=pltpu.CompilerParams(
            dimension_semantics=("arbitrary",), vmem_limit_bytes=VMEM_LIMIT),
        name="in_proj",
    )(xv, gain, w_in)


def _fft_tables(s):
    n2 = FFT_N2
    n1 = s // n2
    s1 = np.arange(n1, dtype=np.int64)
    k1 = np.arange(n1, dtype=np.int64)
    s2 = np.arange(n2, dtype=np.int64)
    idx = (k1[None, :, None] * s1[None, None, :] * n2 + s2[:, None, None] * k1[None, :, None]) % s
    ang = 2.0 * np.pi * idx.astype(np.float64) / s
    g = np.stack([np.cos(ang), -np.sin(ang)], axis=1) / math.sqrt(n1)
    g = g.reshape(n2, 2 * n1, n1)
    k2 = np.arange(n2, dtype=np.int64)
    ang2 = 2.0 * np.pi * ((k2[:, None] * s2[None, :]) % n2).astype(np.float64) / n2
    c2, s2m = np.cos(ang2), np.sin(ang2)
    h = np.block([[c2, s2m], [-s2m, c2]]) / math.sqrt(n2)
    c = np.arange(FNET_GROUP_DIM, dtype=np.int64)
    angc = 2.0 * np.pi * ((c[:, None] * c[None, :]) % FNET_GROUP_DIM).astype(np.float64) / FNET_GROUP_DIM
    cs = np.concatenate([np.cos(angc), np.sin(angc)], axis=0) / math.sqrt(FNET_GROUP_DIM)
    return g.astype(np.float32), h.astype(np.float32), cs.astype(np.float32)


def _fft_kernel(x_ref, g_ref, h_ref, cs_ref, o_ref, xs_scr, a_scr, os_scr,
                *, n1, n2, ng, p1, s2b, k1b):
    p = pl.program_id(2)
    kstride = 2 * n2 + FFT_ROW_PAD

    @pl.when(p < p1)
    def _stage1():
        xs_scr[...] = x_ref[...].reshape(ng, n1 * s2b, LANES)
        for j in range(s2b):
            xj = jnp.concatenate(
                [xs_scr[g, pl.ds(j, n1, stride=s2b), :] for g in range(ng)], axis=1).astype(BF16)
            r = _dot(g_ref[j], xj)
            s2 = p * s2b + j
            for g in range(ng):
                a_scr[g, pl.ds(s2, n1, stride=kstride), :] = r[0:n1, g * LANES:(g + 1) * LANES]
                a_scr[g, pl.ds(n2 + s2, n1, stride=kstride), :] = r[n1:2 * n1, g * LANES:(g + 1) * LANES]

    @pl.when(p >= p1)
    def _stage2():
        for j in range(k1b):
            row0 = pl.multiple_of(((p - p1) * k1b + j) * kstride, SUBLANES)
            a = jnp.concatenate(
                [a_scr[g, pl.ds(row0, 2 * n2), :] for g in range(ng)], axis=1).astype(BF16)
            y = _dot(h_ref[...], a).astype(BF16)
            for g in range(ng):
                yg = jnp.concatenate(
                    [y[0:n2, g * LANES:(g + 1) * LANES], y[n2:2 * n2, g * LANES:(g + 1) * LANES]], axis=1)
                os_scr[g, pl.ds(j, n2, stride=k1b), :] = _dot(yg, cs_ref[...])
        o_ref[...] = os_scr[...].reshape(ng, n2, k1b, LANES)


def _fourier_mix(u):
    ngroups, b, s, c = u.shape
    assert c == FNET_GROUP_DIM == LANES
    n2 = FFT_N2
    n1 = s // n2
    ng = FFT_GROUPS_PER_PASS
    kstride = 2 * n2 + FFT_ROW_PAD

    def vmem_bytes(s2b, k1b):
        f32 = 4 * ng * c
        return (f32 * n1 * kstride + 3 * f32 * n1 * s2b + 3 * f32 * n2 * k1b
                + 2 * 2 * s2b * 2 * n1 * max(n1, LANES))

    s2b, k1b = FFT_S2_BLOCK, FFT_K1_BLOCK
    if vmem_bytes(2 * s2b, 2 * k1b) <= FFT_VMEM_BUDGET and n2 % (2 * s2b) == 0 and n1 % (2 * k1b) == 0:
        s2b, k1b = 2 * s2b, 2 * k1b
    assert n1 * n2 == s and n1 % k1b == 0 and n2 % s2b == 0 and ngroups % ng == 0
    p1, p2 = n2 // s2b, n1 // k1b
    g_np, h_np, cs_np = _fft_tables(s)
    g_t = jnp.asarray(g_np).astype(BF16)
    h_t = jnp.asarray(h_np).astype(BF16)
    cs_t = jnp.asarray(cs_np).astype(BF16)

    f = pl.pallas_call(
        functools.partial(_fft_kernel, n1=n1, n2=n2, ng=ng, p1=p1, s2b=s2b, k1b=k1b),
        grid=(b, ngroups // ng, p1 + p2),
        in_specs=[
            pl.BlockSpec((ng, None, n1, s2b, c),
                         lambda bi, gi, p: (gi, bi, 0, jnp.minimum(p, p1 - 1), 0)),
            pl.BlockSpec((s2b, 2 * n1, n1), lambda bi, gi, p: (jnp.minimum(p, p1 - 1), 0, 0)),
            _const_spec((2 * n2, 2 * n2)),
            _const_spec((2 * FNET_GROUP_DIM, FNET_GROUP_DIM)),
        ],
        out_specs=pl.BlockSpec((ng, None, n2, k1b, c),
                               lambda bi, gi, p: (gi, bi, 0, jnp.maximum(p - p1, 0), 0)),
        out_shape=jax.ShapeDtypeStruct((ngroups, b, n2, n1, c), F32),
        scratch_shapes=[pltpu.VMEM((ng, n1 * s2b, c), F32),
                        pltpu.VMEM((ng, n1 * kstride, c), F32),
                        pltpu.VMEM((ng, n2 * k1b, c), F32)],
        compiler_params=pltpu.CompilerParams(
            dimension_semantics=("arbitrary", "arbitrary", "arbitrary"),
            vmem_limit_bytes=VMEM_LIMIT),
        name="fft",
    )(u.reshape(ngroups, b, n1, n2, c), g_t, h_t, cs_t)
    return f.reshape(ngroups, b, s, c)


def _ssm_tables(lam_re, lam_im, log_dt, b_re, b_im, c_re, c_im, d_skip, seg_len):
    assert PAIR == 2
    ndir, ng, p = lam_re.shape
    hd = b_re.shape[-1]
    r = SCAN_BLOCK
    nq = ng // PAIR
    lam_re = lam_re.astype(F32)
    lam_im = lam_im.astype(F32)
    dt = jnp.exp(log_dt.astype(F32))[..., None]

    def apow(k):
        mag = jnp.exp(lam_re * dt * k)
        return mag * jnp.cos(lam_im * dt * k), mag * jnp.sin(lam_im * dt * k)

    ar, ai = apow(1.0)
    den = lam_re * lam_re + lam_im * lam_im
    qr = ((ar - 1.0) * lam_re + ai * lam_im) / den
    qi = (ai * lam_re - (ar - 1.0) * lam_im) / den
    bre = qr[..., None] * b_re.astype(F32) - qi[..., None] * b_im.astype(F32)
    bim = qr[..., None] * b_im.astype(F32) + qi[..., None] * b_re.astype(F32)
    cr = c_re.astype(F32)
    ci = c_im.astype(F32)

    ks = jnp.arange(r + 1, dtype=F32).reshape(r + 1, 1, 1, 1)
    pr, pi = apow(ks)
    abr = pr[:r, ..., None] * bre - pi[:r, ..., None] * bim
    abi = pr[:r, ..., None] * bim + pi[:r, ..., None] * bre

    def in_rows(x):
        return jnp.transpose(x, (1, 0, 3, 2)).reshape(ng, r * hd, p)

    bp_re = jnp.stack([in_rows(abr[::-1, 0]), in_rows(abr[:, 1])])
    bp_im = jnp.stack([in_rows(abi[::-1, 0]), in_rows(abi[:, 1])])

    klag = jnp.sum(cr[None, :, :, :, :, None] * abr[:, :, :, None, :, :]
                   - ci[None, :, :, :, :, None] * abi[:, :, :, None, :, :], axis=4)
    kzero = jnp.zeros_like(klag[0, 0])
    kin = jnp.stack([
        jnp.stack([(klag[i - j, 0] if i >= j else kzero) + (klag[j - i, 1] if j >= i else kzero)
                   for i in range(r)]) for j in range(r)])
    kin = jnp.transpose(kin, (2, 0, 4, 1, 3)).reshape(ng, r * hd, r * hd)
    dsk = d_skip.astype(F32).reshape(ng, hd)
    kin = kin + jnp.eye(r * hd, dtype=F32)[None] * jnp.tile(dsk, (1, r))[:, None, :]

    def state_out(d, pows):
        er = cr[d][None] * pr[pows, d][:, :, None, :] - ci[d][None] * pi[pows, d][:, :, None, :]
        ei = cr[d][None] * pi[pows, d][:, :, None, :] + ci[d][None] * pr[pows, d][:, :, None, :]
        to_rows = lambda e: jnp.transpose(e, (1, 3, 0, 2)).reshape(ng, p, r * hd)
        return to_rows(er), -to_rows(ei)

    cf_re, cf_im = state_out(0, jnp.arange(1, r + 1))
    cb_re, cb_im = state_out(1, r - jnp.arange(r))

    def pair_bd(m):
        lead = m.shape[:-3]
        rr, cc = m.shape[-2:]
        m = m.reshape(lead + (nq, PAIR, rr, cc))
        m0, m1 = m[..., 0, :, :], m[..., 1, :, :]
        z = jnp.zeros_like(m0)
        return jnp.concatenate([jnp.concatenate([m0, z], axis=-1),
                                jnp.concatenate([z, m1], axis=-1)], axis=-2)

    bq = jnp.concatenate([pair_bd(bp_re), pair_bd(bp_im)], axis=-1).astype(BF16)
    wy = jnp.concatenate([pair_bd(kin), pair_bd(cf_re), pair_bd(cf_im),
                          pair_bd(cb_re), pair_bd(cb_im)], axis=-2).astype(BF16)

    def lanes(v):
        v = v.reshape(ndir, nq, 1, PAIR * p)
        return jnp.broadcast_to(v, (ndir, nq, SUBLANES, PAIR * p))

    a8r, a8i = apow(float(r))
    alr, ali = apow(float(seg_len))
    return bq, wy, lanes(a8r), lanes(a8i), lanes(alr), lanes(ali)


def _scan_kernel(*refs, mc, nq, reverse, chain, emit):
    u8_ref, e_ref, bq_ref, a8re_ref, a8im_ref, alre_ref, alim_ref = refs[:7]
    if emit == "y":
        xo_ref, wy_ref, y8_ref, eo_ref, d_scr, st_scr = refs[7:]
    elif emit == "x":
        x_ref, eo_ref, d_scr, st_scr = refs[7:]
    else:
        eo_ref, d_scr, st_scr = refs[7:]
    i = pl.program_id(0)
    n = pl.num_programs(0)
    rows = mc * SUBLANES
    hl = LANES

    @pl.when(i == 0)
    def _init():
        if not chain:
            st_scr[...] = jnp.zeros(st_scr.shape, F32)
        else:
            row = lax.broadcasted_iota(jnp.int32, (SUBLANES, hl), 0)
            shift = SUBLANES - 1 if reverse else 1
            keep = (row <= SUBLANES - 2) if reverse else (row >= 1)
            for q in range(nq):
                er = pltpu.roll(e_ref[q, :, 0:hl], shift, 0)
                ei = pltpu.roll(e_ref[q, :, hl:2 * hl], shift, 0)
                lr = alre_ref[q]
                li = alim_ref[q]
                xr = jnp.zeros((SUBLANES, hl), F32)
                xi = jnp.zeros((SUBLANES, hl), F32)
                for _ in range(SUBLANES - 1):
                    pr = pltpu.roll(xr, shift, 0)
                    pi = pltpu.roll(xi, shift, 0)
                    xr = jnp.where(keep, lr * pr - li * pi + er, 0.0)
                    xi = jnp.where(keep, lr * pi + li * pr + ei, 0.0)
                st_scr[q, :, 0:hl] = xr
                st_scr[q, :, hl:2 * hl] = xi

    for q0 in range(0, nq, SCAN_INTERLEAVE):
        qs = list(range(q0, min(q0 + SCAN_INTERLEAVE, nq)))
        for k, q in enumerate(qs):
            d_scr[k] = _dot(u8_ref[q], bq_ref[q])
        a_re = [a8re_ref[q] for q in qs]
        a_im = [a8im_ref[q] for q in qs]

        def body(step, carry, a_re=a_re, a_im=a_im, nk=len(qs)):
            m = (mc - 1 - step) if reverse else step
            r = pl.multiple_of(m * SUBLANES, SUBLANES)
            out = []
            for k in range(nk):
                re, im = carry[2 * k], carry[2 * k + 1]
                d_re = d_scr[k, pl.ds(r, SUBLANES), 0:hl]
                d_im = d_scr[k, pl.ds(r, SUBLANES), hl:2 * hl]
                d_scr[k, pl.ds(r, SUBLANES), 0:hl] = re
                d_scr[k, pl.ds(r, SUBLANES), hl:2 * hl] = im
                out.append(a_re[k] * re - a_im[k] * im + d_re)
                out.append(a_re[k] * im + a_im[k] * re + d_im)
            return tuple(out)

        init = []
        for q in qs:
            init += [st_scr[q, :, 0:hl], st_scr[q, :, hl:2 * hl]]
        fin = lax.fori_loop(0, mc, body, tuple(init), unroll=2)
        for k, q in enumerate(qs):
            st_scr[q, :, 0:hl] = fin[2 * k]
            st_scr[q, :, hl:2 * hl] = fin[2 * k + 1]
            if emit == "x":
                x_ref[q] = d_scr[k].astype(BF16)
            elif emit == "y":
                lhs = jnp.concatenate([u8_ref[q], xo_ref[q], d_scr[k].astype(BF16)], axis=1)
                y8_ref[q] = _dot(lhs, wy_ref[q]).astype(BF16)

    @pl.when(i == n - 1)
    def _fin():
        eo_ref[...] = st_scr[...]


def _scan(u8, e_in, bq, a8re, a8im, alre, alim, mc, reverse, chain, emit, x_other=None, wy=None):
    nq = bq.shape[0]
    mtot = u8.shape[1] // NB
    n = mtot // mc
    rows = mc * NB
    order = (lambda i: n - 1 - i) if reverse else (lambda i: i)
    st_shape = (nq, SUBLANES, 2 * LANES)
    row_spec = pl.BlockSpec((nq, rows, 2 * LANES), lambda i: (0, order(i), 0))
    row_shape = jax.ShapeDtypeStruct((nq, mtot * NB, 2 * LANES), BF16)
    in_specs = [row_spec, _const_spec(st_shape), _const_spec(bq.shape),
                _const_spec(a8re.shape), _const_spec(a8im.shape),
                _const_spec(alre.shape), _const_spec(alim.shape)]
    args = [u8, e_in, bq, a8re, a8im, alre, alim]
    out_specs = [_const_spec(st_shape)]
    out_shape = [jax.ShapeDtypeStruct(st_shape, F32)]
    if emit == "y":
        in_specs += [row_spec, _const_spec(wy.shape)]
        args += [x_other, wy]
    if emit is not None:
        out_specs = [row_spec] + out_specs
        out_shape = [row_shape] + out_shape
    kern = functools.partial(_scan_kernel, mc=mc, nq=nq, reverse=reverse, chain=chain, emit=emit)
    return pl.pallas_call(
        kern,
        grid=(n,),
        in_specs=in_specs,
        out_specs=out_specs,
        out_shape=out_shape,
        scratch_shapes=[pltpu.VMEM((SCAN_INTERLEAVE, rows, 2 * LANES), F32),
                        pltpu.VMEM(st_shape, F32)],
        compiler_params=pltpu.CompilerParams(
            dimension_semantics=("arbitrary",), vmem_limit_bytes=VMEM_LIMIT),
        name="ssm_scan_" + ("bwd" if reverse else "fwd") + ("_chain" if chain else "")
             + ("_" + emit if emit else "_end"),
    )(*args)


def _ssm(u8, tables, chained, mc):
    bq, wy, a8re, a8im, alre, alim = tables
    nq = bq.shape[1]
    ef = eb = jnp.zeros((nq, SUBLANES, 2 * LANES), F32)
    par = lambda dr: (bq[dr], a8re[dr], a8im[dr], alre[dr], alim[dr])
    if chained:
        (ef,) = _scan(u8, ef, *par(0), mc, False, False, None)
        (eb,) = _scan(u8, eb, *par(1), mc, True, False, None)
    xf, _ = _scan(u8, ef, *par(0), mc, False, chained, "x")
    y8, _ = _scan(u8, eb, *par(1), mc, True, chained, "y", xf, wy)
    return y8


def _ffn_chunks(hidden, step=1024):
    edges = list(range(0, hidden, step)) + [hidden]
    return list(zip(edges[:-1], edges[1:]))


def _mix_kernel(x_ref, f_ref, y8_ref, sa_ref, sb_ref, gpost_ref,
                wfo_ref, wval_ref, wgate_ref, wout_ref, o_ref, y_scr, *, tt, d, fw, sw):
    rows = NB * tt
    qpb = GROUPS_PER_BLOCK // PAIR
    for cb in range(sw // LANES):
        vs = [y8_ref[cb * qpb + gl // PAIR, :, (gl % PAIR) * LANES:(gl % PAIR + 1) * LANES].astype(F32)
              for gl in range(GROUPS_PER_BLOCK)]
        unfolded = _transpose_pieces(vs)
        for i in range(SCAN_BLOCK):
            for m in range(tt // SCAN_BLOCK):
                y_scr[cb, pl.ds((SCAN_BLOCK * m + i) * NB, NB), :] = unfolded[i][m * NB:(m + 1) * NB, :]
    ys = jnp.concatenate(
        [jnp.concatenate(
            [y_scr[cb, pl.ds(b, tt, stride=NB), :] for b in range(NB)], axis=0)
         for cb in range(sw // LANES)], axis=1)
    z = jax.nn.gelu(ys).astype(BF16)
    f = jnp.concatenate(
        [f_ref[g].reshape(rows, LANES) for g in range(fw // LANES)], axis=1).astype(BF16)
    br_a = _dot(f, wfo_ref[...])
    br_b = _dot(z, wval_ref[...]) * jax.nn.sigmoid(_dot(z, wgate_ref[...]))
    sa = sa_ref[...].reshape(rows, d).astype(F32)
    sb = sb_ref[...].reshape(rows, d).astype(F32)
    merged = (sa * br_a + sb * br_b).astype(BF16)
    m = _dot(merged, wout_ref[...])
    o_ref[...] = (x_ref[...].reshape(rows, d) + _rms(m, gpost_ref[...])).reshape(NB, tt, d)


def _mix(xv, f, y8, sa, sb, gpost, wfo, wval, wgate, wout, tt):
    nb, sseg, d = xv.shape
    fw = f.shape[0] * LANES
    nq = y8.shape[0]
    sw = nq * PAIR * SSM_GROUP_DIM
    n = sseg // tt
    assert tt % SCAN_BLOCK == 0
    kern = functools.partial(_mix_kernel, tt=tt, d=d, fw=fw, sw=sw)
    tok = lambda width: pl.BlockSpec((NB, tt, width), lambda i: (0, i, 0))
    tmaj = pl.BlockSpec((nq, tt, PAIR * LANES), lambda i: (0, i, 0))
    return pl.pallas_call(
        kern,
        grid=(n,),
        in_specs=[tok(d), pl.BlockSpec((fw // LANES, NB, tt, LANES), lambda i: (0, 0, i, 0)),
                  tmaj, tok(d), tok(d), _const_spec((1, d)),
                  _const_spec(wfo.shape), _const_spec(wval.shape), _const_spec(wgate.shape),
                  _const_spec(wout.shape)],
        out_specs=tok(d),
        out_shape=jax.ShapeDtypeStruct((NB, sseg, d), F32),
        scratch_shapes=[pltpu.VMEM((sw // LANES, NB * tt, LANES), F32)],
        compiler_params=pltpu.CompilerParams(
            dimension_semantics=("arbitrary",), vmem_limit_bytes=VMEM_LIMIT),
        name="mix",
    )(xv, f, y8, sa, sb, gpost, wfo, wval, wgate, wout)


def _ffn_kernel(x_ref, gpre_ref, gpost_ref, wg_ref, wu_ref, wd_ref, o_ref, *, tt, d, hidden):
    rows = NB * tt
    x1 = x_ref[...].reshape(rows, d)
    h2 = _rms(x1, gpre_ref[...]).astype(BF16)
    acc = None
    for c0, c1 in _ffn_chunks(hidden):
        g = _dot(h2, wg_ref[:, c0:c1])
        u = _dot(h2, wu_ref[:, c0:c1])
        part = _dot((jax.nn.silu(g) * u).astype(BF16), wd_ref[c0:c1, :])
        acc = part if acc is None else acc + part
    o_ref[...] = (x1 + _rms(acc, gpost_ref[...])).reshape(NB, tt, d)


def _ffn(xv, gpre, gpost, wg, wu, wd, tt):
    nb, sseg, d = xv.shape
    hidden = wg.shape[-1]
    tok = pl.BlockSpec((NB, tt, d), lambda i: (0, i, 0))
    return pl.pallas_call(
        functools.partial(_ffn_kernel, tt=tt, d=d, hidden=hidden),
        grid=(sseg // tt,),
        in_specs=[tok, _const_spec((1, d)), _const_spec((1, d)),
                  _const_spec(wg.shape), _const_spec(wu.shape), _const_spec(wd.shape)],
        out_specs=tok,
        out_shape=jax.ShapeDtypeStruct((NB, sseg, d), F32),
        compiler_params=pltpu.CompilerParams(
            dimension_semantics=("arbitrary",), vmem_limit_bytes=VMEM_LIMIT),
        name="ffn",
    )(xv, gpre, gpost, wg, wu, wd)


def _pick(total, want):
    t = min(want, total)
    assert total % t == 0, (total, t)
    return t


def _encoder_layer(x, p):
    b, s, d = x.shape
    assert b in (1, NB), "one sequence (split in NB segments) or NB sequences"
    sseg = (b * s) // NB
    fw = p["w_fnet_out"].shape[0]
    sw = p["w_glu_val"].shape[0]
    xv = x.reshape(NB, sseg, d)
    tt = _pick(sseg, TOKEN_TILE_T)

    uf, u8, sa, sb = _in_proj(xv, p["norm_mix_pre"], p["w_in"], fw, sw, tt)
    ng = fw // LANES
    f = _fourier_mix(uf.reshape(ng, b, s, LANES)).reshape(ng, NB, sseg, LANES)

    tables = _ssm_tables(p["lam_re"], p["lam_im"], p["log_dt"], p["b_re"], p["b_im"],
                         p["c_re"], p["c_im"], p["d_skip"], sseg)
    ys = _ssm(u8, tables, chained=(b == 1), mc=_pick(sseg // SCAN_BLOCK, 32))

    x1 = _mix(xv, f, ys, sa, sb, p["norm_mix_post"],
              p["w_fnet_out"], p["w_glu_val"], p["w_glu_gate"], p["w_out"], tt)
    y = _ffn(x1, p["norm_ffn_pre"], p["norm_ffn_post"],
             p["w_ffn_gate"], p["w_ffn_up"], p["w_ffn_down"], tt)
    return y.reshape(b, s, d)


def _layer_params(l, norm_mix_pre, norm_mix_post, norm_ffn_pre, norm_ffn_post, w_in, w_fnet_out,
                  lam_re, lam_im, log_dt, b_re, b_im, c_re, c_im, d_skip, w_glu_val, w_glu_gate,
                  w_out, w_ffn_gate, w_ffn_up, w_ffn_down):
    row = lambda v: v[l].astype(F32).reshape(1, -1)
    w = lambda v: v[l].astype(BF16)
    return dict(
        norm_mix_pre=row(norm_mix_pre), norm_mix_post=row(norm_mix_post),
        norm_ffn_pre=row(norm_ffn_pre), norm_ffn_post=row(norm_ffn_post),
        w_in=w(w_in), w_fnet_out=w(w_fnet_out),
        lam_re=lam_re[l], lam_im=lam_im[l], log_dt=log_dt[l],
        b_re=b_re[l], b_im=b_im[l], c_re=c_re[l], c_im=c_im[l], d_skip=row(d_skip),
        w_glu_val=w(w_glu_val), w_glu_gate=w(w_glu_gate), w_out=w(w_out),
        w_ffn_gate=w(w_ffn_gate), w_ffn_up=w(w_ffn_up), w_ffn_down=w(w_ffn_down))


def kernel(x_prompt, x_sample, norm_mix_pre, norm_mix_post, norm_ffn_pre, norm_ffn_post, w_in, w_fnet_out, lam_re, lam_im, log_dt, b_re, b_im, c_re, c_im, d_skip, w_glu_val, w_glu_gate, w_out, w_ffn_gate, w_ffn_up, w_ffn_down):
    weights = (norm_mix_pre, norm_mix_post, norm_ffn_pre, norm_ffn_post, w_in, w_fnet_out,
               lam_re, lam_im, log_dt, b_re, b_im, c_re, c_im, d_skip, w_glu_val, w_glu_gate,
               w_out, w_ffn_gate, w_ffn_up, w_ffn_down)
    depth = w_in.shape[0]
    outs = []
    for x in (x_prompt, x_sample):
        for l in range(depth):
            x = _encoder_layer(x, _layer_params(l, *weights))
        outs.append(x)
    return tuple(outs)
```

```python
import functools
import math

import numpy as np
import jax
import jax.numpy as jnp
from jax import lax
from jax.experimental import pallas as pl
from jax.experimental.pallas import tpu as pltpu

F32 = jnp.float32
BF16 = jnp.bfloat16

EPS = 1e-6
SUBLANES = 8
LANES = 128
NB = SUBLANES

FNET_GROUP_DIM = 128
SSM_GROUP_DIM = 16
GROUPS_PER_BLOCK = LANES // SSM_GROUP_DIM
SCAN_BLOCK = LANES // SSM_GROUP_DIM
PAIR = 2
SCAN_INTERLEAVE = 4
SCAN_CHUNK_BLOCKS = 64

TOKEN_TILE_T = 128

FFT_N2 = 128
FFT_S2_BLOCK = 16
FFT_K1_BLOCK = 8
FFT_VMEM_BUDGET = 48 * 1024 * 1024
FFT_GROUPS_PER_PASS = 2
FFT_ROW_PAD = SUBLANES
VMEM_LIMIT = 56 * 1024 * 1024


def _dot(a, b):
    return jnp.dot(a, b, preferred_element_type=F32)


def _rms(x, g):
    return x * lax.rsqrt(jnp.mean(x * x, axis=-1, keepdims=True) + EPS) * g


def _const_spec(shape):
    nd = len(shape)
    return pl.BlockSpec(shape, lambda *_: (0,) * nd, pipeline_mode=pl.Buffered(1))


def _transpose_pieces(vs):
    n = GROUPS_PER_BLOCK
    rows = vs[0].shape[0]
    piece = lax.broadcasted_iota(jnp.int32, (rows, LANES), 1) // SSM_GROUP_DIM
    cur = list(vs)
    d = 1
    while d < n:
        upper = (piece & d) != 0
        nxt = [None] * n
        for c in range(n):
            if c & d == 0:
                lo, hi = cur[c], cur[c | d]
                nxt[c] = jnp.where(upper, pltpu.roll(hi, d * SSM_GROUP_DIM, 1), lo)
                nxt[c | d] = jnp.where(upper, hi, pltpu.roll(lo, LANES - d * SSM_GROUP_DIM, 1))
        cur = nxt
        d *= 2
    return cur


def _in_proj_kernel(x_ref, g_ref, w_ref, uf_ref, u8_ref, sa_ref, sb_ref, us_scr, *, tt, fw, sw, d):
    rows = NB * tt
    o = fw + sw
    x = x_ref[...].reshape(rows, d)
    h = _rms(x, g_ref[...]).astype(BF16)
    uf = _dot(h, w_ref[:, 0:fw])
    for g in range(fw // LANES):
        uf_ref[g] = uf[:, g * LANES:(g + 1) * LANES].reshape(NB, tt, LANES)
    us = _dot(h, w_ref[:, fw:fw + sw])
    sa_ref[...] = jax.nn.sigmoid(_dot(h, w_ref[:, o:o + d])).reshape(NB, tt, d).astype(BF16)
    sb_ref[...] = jax.nn.sigmoid(_dot(h, w_ref[:, o + d:o + 2 * d])).reshape(NB, tt, d).astype(BF16)
    for cb in range(sw // LANES):
        for b in range(NB):
            us_scr[cb, pl.ds(b, tt, stride=NB), :] = us[b * tt:(b + 1) * tt, cb * LANES:(cb + 1) * LANES]
    for cb in range(sw // LANES):
        vs = [jnp.concatenate([us_scr[cb, pl.ds((SCAN_BLOCK * m + j) * NB, NB), :]
                               for m in range(tt // SCAN_BLOCK)], axis=0)
              for j in range(SCAN_BLOCK)]
        folded = _transpose_pieces(vs)
        for gl in range(GROUPS_PER_BLOCK):
            g = cb * GROUPS_PER_BLOCK + gl
            u8_ref[g // PAIR, :, (g % PAIR) * LANES:(g % PAIR + 1) * LANES] = folded[gl].astype(BF16)


def _in_proj(xv, gain, w_in, fw, sw, tt):
    nb, sseg, d = xv.shape
    n = sseg // tt
    nq = sw // (PAIR * SSM_GROUP_DIM)
    assert tt % SCAN_BLOCK == 0
    kern = functools.partial(_in_proj_kernel, tt=tt, fw=fw, sw=sw, d=d)
    return pl.pallas_call(
        kern,
        grid=(n,),
        in_specs=[
            pl.BlockSpec((NB, tt, d), lambda i: (0, i, 0)),
            _const_spec((1, d)),
            _const_spec(w_in.shape),
        ],
        out_specs=[
            pl.BlockSpec((fw // LANES, NB, tt, LANES), lambda i: (0, 0, i, 0)),
            pl.BlockSpec((nq, tt, PAIR * LANES), lambda i: (0, i, 0)),
            pl.BlockSpec((NB, tt, d), lambda i: (0, i, 0)),
            pl.BlockSpec((NB, tt, d), lambda i: (0, i, 0)),
        ],
        out_shape=[
            jax.ShapeDtypeStruct((fw // LANES, NB, sseg, LANES), F32),
            jax.ShapeDtypeStruct((nq, sseg, PAIR * LANES), BF16),
            jax.ShapeDtypeStruct((NB, sseg, d), BF16),
            jax.ShapeDtypeStruct((NB, sseg, d), BF16),
        ],
        scratch_shapes=[pltpu.VMEM((sw // LANES, NB * tt, LANES), F32)],
        compiler_params=pltpu.CompilerParams(
            dimension_semantics=("arbitrary",), vmem_limit_bytes=VMEM_LIMIT),
        name="in_proj",
    )(xv, gain, w_in)


def _fft_tables(s):
    n2 = FFT_N2
    n1 = s // n2
    s1 = np.arange(n1, dtype=np.int64)
    k1 = np.arange(n1, dtype=np.int64)
    s2 = np.arange(n2, dtype=np.int64)
    idx = (k1[None, :, None] * s1[None, None, :] * n2 + s2[:, None, None] * k1[None, :, None]) % s
    ang = 2.0 * np.pi * idx.astype(np.float64) / s
    g = np.stack([np.cos(ang), -np.sin(ang)], axis=1) / math.sqrt(n1)
    g = g.reshape(n2, 2 * n1, n1)
    k2 = np.arange(n2, dtype=np.int64)
    ang2 = 2.0 * np.pi * ((k2[:, None] * s2[None, :]) % n2).astype(np.float64) / n2
    c2, s2m = np.cos(ang2), np.sin(ang2)
    h = np.block([[c2, s2m], [-s2m, c2]]) / math.sqrt(n2)
    c = np.arange(FNET_GROUP_DIM, dtype=np.int64)
    angc = 2.0 * np.pi * ((c[:, None] * c[None, :]) % FNET_GROUP_DIM).astype(np.float64) / FNET_GROUP_DIM
    cs = np.concatenate([np.cos(angc), np.sin(angc)], axis=0) / math.sqrt(FNET_GROUP_DIM)
    return g.astype(np.float32), h.astype(np.float32), cs.astype(np.float32)


def _fft_kernel(x_ref, g_ref, h_ref, cs_ref, o_ref, xs_scr, a_scr, os_scr,
                *, n1, n2, ng, p1, s2b, k1b):
    p = pl.program_id(2)
    kstride = 2 * n2 + FFT_ROW_PAD

    @pl.when(p < p1)
    def _stage1():
        xs_scr[...] = x_ref[...].reshape(ng, n1 * s2b, LANES)
        for j in range(s2b):
            xj = jnp.concatenate(
                [xs_scr[g, pl.ds(j, n1, stride=s2b), :] for g in range(ng)], axis=1).astype(BF16)
            r = _dot(g_ref[j], xj)
            s2 = p * s2b + j
            for g in range(ng):
                a_scr[g, pl.ds(s2, n1, stride=kstride), :] = r[0:n1, g * LANES:(g + 1) * LANES]
                a_scr[g, pl.ds(n2 + s2, n1, stride=kstride), :] = r[n1:2 * n1, g * LANES:(g + 1) * LANES]

    @pl.when(p >= p1)
    def _stage2():
        for j in range(k1b):
            row0 = pl.multiple_of(((p - p1) * k1b + j) * kstride, SUBLANES)
            a = jnp.concatenate(
                [a_scr[g, pl.ds(row0, 2 * n2), :] for g in range(ng)], axis=1).astype(BF16)
            y = _dot(h_ref[...], a).astype(BF16)
            for g in range(ng):
                yg = jnp.concatenate(
                    [y[0:n2, g * LANES:(g + 1) * LANES], y[n2:2 * n2, g * LANES:(g + 1) * LANES]], axis=1)
                os_scr[g, pl.ds(j, n2, stride=k1b), :] = _dot(yg, cs_ref[...])
        o_ref[...] = os_scr[...].reshape(ng, n2, k1b, LANES)


def _fourier_mix(u):
    ngroups, b, s, c = u.shape
    assert c == FNET_GROUP_DIM == LANES
    n2 = FFT_N2
    n1 = s // n2
    ng = FFT_GROUPS_PER_PASS
    kstride = 2 * n2 + FFT_ROW_PAD

    def vmem_bytes(s2b, k1b):
        f32 = 4 * ng * c
        return (f32 * n1 * kstride + 3 * f32 * n1 * s2b + 3 * f32 * n2 * k1b
                + 2 * 2 * s2b * 2 * n1 * max(n1, LANES))

    s2b, k1b = FFT_S2_BLOCK, FFT_K1_BLOCK
    while (vmem_bytes(2 * s2b, 2 * k1b) <= FFT_VMEM_BUDGET
           and n2 % (2 * s2b) == 0 and n1 % (2 * k1b) == 0):
        s2b, k1b = 2 * s2b, 2 * k1b
    assert n1 * n2 == s and n1 % k1b == 0 and n2 % s2b == 0 and ngroups % ng == 0
    p1, p2 = n2 // s2b, n1 // k1b
    g_np, h_np, cs_np = _fft_tables(s)
    g_t = jnp.asarray(g_np).astype(BF16)
    h_t = jnp.asarray(h_np).astype(BF16)
    cs_t = jnp.asarray(cs_np).astype(BF16)

    f = pl.pallas_call(
        functools.partial(_fft_kernel, n1=n1, n2=n2, ng=ng, p1=p1, s2b=s2b, k1b=k1b),
        grid=(b, ngroups // ng, p1 + p2),
        in_specs=[
            pl.BlockSpec((ng, None, n1, s2b, c),
                         lambda bi, gi, p: (gi, bi, 0, jnp.minimum(p, p1 - 1), 0)),
            pl.BlockSpec((s2b, 2 * n1, n1), lambda bi, gi, p: (jnp.minimum(p, p1 - 1), 0, 0)),
            _const_spec((2 * n2, 2 * n2)),
            _const_spec((2 * FNET_GROUP_DIM, FNET_GROUP_DIM)),
        ],
        out_specs=pl.BlockSpec((ng, None, n2, k1b, c),
                               lambda bi, gi, p: (gi, bi, 0, jnp.maximum(p - p1, 0), 0)),
        out_shape=jax.ShapeDtypeStruct((ngroups, b, n2, n1, c), F32),
        scratch_shapes=[pltpu.VMEM((ng, n1 * s2b, c), F32),
                        pltpu.VMEM((ng, n1 * kstride, c), F32),
                        pltpu.VMEM((ng, n2 * k1b, c), F32)],
        compiler_params=pltpu.CompilerParams(
            dimension_semantics=("arbitrary", "arbitrary", "arbitrary"),
            vmem_limit_bytes=VMEM_LIMIT),
        name="fft",
    )(u.reshape(ngroups, b, n1, n2, c), g_t, h_t, cs_t)
    return f.reshape(ngroups, b, s, c)


def _ssm_tables(lam_re, lam_im, log_dt, b_re, b_im, c_re, c_im, d_skip, seg_len):
    assert PAIR == 2
    ndir, ng, p = lam_re.shape
    hd = b_re.shape[-1]
    r = SCAN_BLOCK
    nq = ng // PAIR
    lam_re = lam_re.astype(F32)
    lam_im = lam_im.astype(F32)
    dt = jnp.exp(log_dt.astype(F32))[..., None]

    def apow(k):
        mag = jnp.exp(lam_re * dt * k)
        return mag * jnp.cos(lam_im * dt * k), mag * jnp.sin(lam_im * dt * k)

    ar, ai = apow(1.0)
    den = lam_re * lam_re + lam_im * lam_im
    qr = ((ar - 1.0) * lam_re + ai * lam_im) / den
    qi = (ai * lam_re - (ar - 1.0) * lam_im) / den
    bre = qr[..., None] * b_re.astype(F32) - qi[..., None] * b_im.astype(F32)
    bim = qr[..., None] * b_im.astype(F32) + qi[..., None] * b_re.astype(F32)
    cr = c_re.astype(F32)
    ci = c_im.astype(F32)

    ks = jnp.arange(r + 1, dtype=F32).reshape(r + 1, 1, 1, 1)
    pr, pi = apow(ks)
    abr = pr[:r, ..., None] * bre - pi[:r, ..., None] * bim
    abi = pr[:r, ..., None] * bim + pi[:r, ..., None] * bre

    def in_rows(x):
        return jnp.transpose(x, (1, 0, 3, 2)).reshape(ng, r * hd, p)

    bp_re = jnp.stack([in_rows(abr[::-1, 0]), in_rows(abr[:, 1])])
    bp_im = jnp.stack([in_rows(abi[::-1, 0]), in_rows(abi[:, 1])])

    klag = jnp.sum(cr[None, :, :, :, :, None] * abr[:, :, :, None, :, :]
                   - ci[None, :, :, :, :, None] * abi[:, :, :, None, :, :], axis=4)
    kzero = jnp.zeros_like(klag[0, 0])
    kin = jnp.stack([
        jnp.stack([(klag[i - j, 0] if i >= j else kzero) + (klag[j - i, 1] if j >= i else kzero)
                   for i in range(r)]) for j in range(r)])
    kin = jnp.transpose(kin, (2, 0, 4, 1, 3)).reshape(ng, r * hd, r * hd)
    dsk = d_skip.astype(F32).reshape(ng, hd)
    kin = kin + jnp.eye(r * hd, dtype=F32)[None] * jnp.tile(dsk, (1, r))[:, None, :]

    def state_out(d, pows):
        er = cr[d][None] * pr[pows, d][:, :, None, :] - ci[d][None] * pi[pows, d][:, :, None, :]
        ei = cr[d][None] * pi[pows, d][:, :, None, :] + ci[d][None] * pr[pows, d][:, :, None, :]
        to_rows = lambda e: jnp.transpose(e, (1, 3, 0, 2)).reshape(ng, p, r * hd)
        return to_rows(er), -to_rows(ei)

    cf_re, cf_im = state_out(0, jnp.arange(1, r + 1))
    cb_re, cb_im = state_out(1, r - jnp.arange(r))

    def pair_bd(m):
        lead = m.shape[:-3]
        rr, cc = m.shape[-2:]
        m = m.reshape(lead + (nq, PAIR, rr, cc))
        m0, m1 = m[..., 0, :, :], m[..., 1, :, :]
        z = jnp.zeros_like(m0)
        return jnp.concatenate([jnp.concatenate([m0, z], axis=-1),
                                jnp.concatenate([z, m1], axis=-1)], axis=-2)

    bq = jnp.concatenate([pair_bd(bp_re), pair_bd(bp_im)], axis=-1).astype(BF16)
    wy = jnp.concatenate([pair_bd(kin), pair_bd(cf_re), pair_bd(cf_im),
                          pair_bd(cb_re), pair_bd(cb_im)], axis=-2).astype(BF16)

    def lanes(v):
        v = v.reshape(ndir, nq, 1, PAIR * p)
        return jnp.broadcast_to(v, (ndir, nq, SUBLANES, PAIR * p))

    a8r, a8i = apow(float(r))
    alr, ali = apow(float(seg_len))
    return bq, wy, lanes(a8r), lanes(a8i), lanes(alr), lanes(ali)


def _scan_kernel(*refs, mc, nq, reverse, chain, emit):
    u8_ref, e_ref, bq_ref, a8re_ref, a8im_ref, alre_ref, alim_ref = refs[:7]
    if emit == "y":
        xo_ref, wy_ref, y8_ref, eo_ref, d_scr, st_scr = refs[7:]
    elif emit == "x":
        x_ref, eo_ref, d_scr, st_scr = refs[7:]
    else:
        eo_ref, d_scr, st_scr = refs[7:]
    i = pl.program_id(0)
    n = pl.num_programs(0)
    rows = mc * SUBLANES
    hl = LANES

    @pl.when(i == 0)
    def _init():
        if not chain:
            st_scr[...] = jnp.zeros(st_scr.shape, F32)
        else:
            row = lax.broadcasted_iota(jnp.int32, (SUBLANES, hl), 0)
            shift = SUBLANES - 1 if reverse else 1
            keep = (row <= SUBLANES - 2) if reverse else (row >= 1)
            for q in range(nq):
                er = pltpu.roll(e_ref[q, :, 0:hl], shift, 0)
                ei = pltpu.roll(e_ref[q, :, hl:2 * hl], shift, 0)
                lr = alre_ref[q]
                li = alim_ref[q]
                xr = jnp.zeros((SUBLANES, hl), F32)
                xi = jnp.zeros((SUBLANES, hl), F32)
                for _ in range(SUBLANES - 1):
                    pr = pltpu.roll(xr, shift, 0)
                    pi = pltpu.roll(xi, shift, 0)
                    xr = jnp.where(keep, lr * pr - li * pi + er, 0.0)
                    xi = jnp.where(keep, lr * pi + li * pr + ei, 0.0)
                st_scr[q, :, 0:hl] = xr
                st_scr[q, :, hl:2 * hl] = xi

    for q0 in range(0, nq, SCAN_INTERLEAVE):
        qs = list(range(q0, min(q0 + SCAN_INTERLEAVE, nq)))
        for k, q in enumerate(qs):
            d_scr[k] = _dot(u8_ref[q], bq_ref[q])
        a_re = [a8re_ref[q] for q in qs]
        a_im = [a8im_ref[q] for q in qs]

        def body(step, carry, a_re=a_re, a_im=a_im, nk=len(qs)):
            m = (mc - 1 - step) if reverse else step
            r = pl.multiple_of(m * SUBLANES, SUBLANES)
            out = []
            for k in range(nk):
                re, im = carry[2 * k], carry[2 * k + 1]
                d_re = d_scr[k, pl.ds(r, SUBLANES), 0:hl]
                d_im = d_scr[k, pl.ds(r, SUBLANES), hl:2 * hl]
                d_scr[k, pl.ds(r, SUBLANES), 0:hl] = re
                d_scr[k, pl.ds(r, SUBLANES), hl:2 * hl] = im
                out.append(a_re[k] * re - a_im[k] * im + d_re)
                out.append(a_re[k] * im + a_im[k] * re + d_im)
            return tuple(out)

        init = []
        for q in qs:
            init += [st_scr[q, :, 0:hl], st_scr[q, :, hl:2 * hl]]
        fin = lax.fori_loop(0, mc, body, tuple(init), unroll=2)
        for k, q in enumerate(qs):
            st_scr[q, :, 0:hl] = fin[2 * k]
            st_scr[q, :, hl:2 * hl] = fin[2 * k + 1]
            if emit == "x":
                x_ref[q] = d_scr[k].astype(BF16)
            elif emit == "y":
                lhs = jnp.concatenate([u8_ref[q], xo_ref[q], d_scr[k].astype(BF16)], axis=1)
                y8_ref[q] = _dot(lhs, wy_ref[q]).astype(BF16)

    @pl.when(i == n - 1)
    def _fin():
        eo_ref[...] = st_scr[...]


def _scan(u8, e_in, bq, a8re, a8im, alre, alim, mc, reverse, chain, emit, x_other=None, wy=None):
    nq = bq.shape[0]
    mtot = u8.shape[1] // NB
    n = mtot // mc
    rows = mc * NB
    order = (lambda i: n - 1 - i) if reverse else (lambda i: i)
    st_shape = (nq, SUBLANES, 2 * LANES)
    row_spec = pl.BlockSpec((nq, rows, 2 * LANES), lambda i: (0, order(i), 0))
    row_shape = jax.ShapeDtypeStruct((nq, mtot * NB, 2 * LANES), BF16)
    in_specs = [row_spec, _const_spec(st_shape), _const_spec(bq.shape),
                _const_spec(a8re.shape), _const_spec(a8im.shape),
                _const_spec(alre.shape), _const_spec(alim.shape)]
    args = [u8, e_in, bq, a8re, a8im, alre, alim]
    out_specs = [_const_spec(st_shape)]
    out_shape = [jax.ShapeDtypeStruct(st_shape, F32)]
    if emit == "y":
        in_specs += [row_spec, _const_spec(wy.shape)]
        args += [x_other, wy]
    if emit is not None:
        out_specs = [row_spec] + out_specs
        out_shape = [row_shape] + out_shape
    kern = functools.partial(_scan_kernel, mc=mc, nq=nq, reverse=reverse, chain=chain, emit=emit)
    return pl.pallas_call(
        kern,
        grid=(n,),
        in_specs=in_specs,
        out_specs=out_specs,
        out_shape=out_shape,
        scratch_shapes=[pltpu.VMEM((SCAN_INTERLEAVE, rows, 2 * LANES), F32),
                        pltpu.VMEM(st_shape, F32)],
        compiler_params=pltpu.CompilerParams(
            dimension_semantics=("arbitrary",), vmem_limit_bytes=VMEM_LIMIT),
        name="ssm_scan_" + ("bwd" if reverse else "fwd") + ("_chain" if chain else "")
             + ("_" + emit if emit else "_end"),
    )(*args)


def _ssm(u8, tables, chained, mc):
    bq, wy, a8re, a8im, alre, alim = tables
    nq = bq.shape[1]
    ef = eb = jnp.zeros((nq, SUBLANES, 2 * LANES), F32)
    par = lambda dr: (bq[dr], a8re[dr], a8im[dr], alre[dr], alim[dr])
    if chained:
        (ef,) = _scan(u8, ef, *par(0), mc, False, False, None)
        (eb,) = _scan(u8, eb, *par(1), mc, True, False, None)
    xf, _ = _scan(u8, ef, *par(0), mc, False, chained, "x")
    y8, _ = _scan(u8, eb, *par(1), mc, True, chained, "y", xf, wy)
    return y8


def _ffn_chunks(hidden, step=1024):
    edges = list(range(0, hidden, step)) + [hidden]
    return list(zip(edges[:-1], edges[1:]))


def _mix_kernel(x_ref, f_ref, y8_ref, sa_ref, sb_ref, gpost_ref,
                wfo_ref, wval_ref, wgate_ref, wout_ref, o_ref, y_scr, *, tt, d, fw, sw):
    rows = NB * tt
    qpb = GROUPS_PER_BLOCK // PAIR
    for cb in range(sw // LANES):
        vs = [y8_ref[cb * qpb + gl // PAIR, :, (gl % PAIR) * LANES:(gl % PAIR + 1) * LANES].astype(F32)
              for gl in range(GROUPS_PER_BLOCK)]
        unfolded = _transpose_pieces(vs)
        for i in range(SCAN_BLOCK):
            for m in range(tt // SCAN_BLOCK):
                y_scr[cb, pl.ds((SCAN_BLOCK * m + i) * NB, NB), :] = unfolded[i][m * NB:(m + 1) * NB, :]
    ys = jnp.concatenate(
        [jnp.concatenate(
            [y_scr[cb, pl.ds(b, tt, stride=NB), :] for b in range(NB)], axis=0)
         for cb in range(sw // LANES)], axis=1)
    z = jax.nn.gelu(ys).astype(BF16)
    f = jnp.concatenate(
        [f_ref[g].reshape(rows, LANES) for g in range(fw // LANES)], axis=1).astype(BF16)
    br_a = _dot(f, wfo_ref[...])
    br_b = _dot(z, wval_ref[...]) * jax.nn.sigmoid(_dot(z, wgate_ref[...]))
    sa = sa_ref[...].reshape(rows, d).astype(F32)
    sb = sb_ref[...].reshape(rows, d).astype(F32)
    merged = (sa * br_a + sb * br_b).astype(BF16)
    m = _dot(merged, wout_ref[...])
    o_ref[...] = (x_ref[...].reshape(rows, d) + _rms(m, gpost_ref[...])).reshape(NB, tt, d)


def _mix(xv, f, y8, sa, sb, gpost, wfo, wval, wgate, wout, tt):
    nb, sseg, d = xv.shape
    fw = f.shape[0] * LANES
    nq = y8.shape[0]
    sw = nq * PAIR * SSM_GROUP_DIM
    n = sseg // tt
    assert tt % SCAN_BLOCK == 0
    kern = functools.partial(_mix_kernel, tt=tt, d=d, fw=fw, sw=sw)
    tok = lambda width: pl.BlockSpec((NB, tt, width), lambda i: (0, i, 0))
    tmaj = pl.BlockSpec((nq, tt, PAIR * LANES), lambda i: (0, i, 0))
    return pl.pallas_call(
        kern,
        grid=(n,),
        in_specs=[tok(d), pl.BlockSpec((fw // LANES, NB, tt, LANES), lambda i: (0, 0, i, 0)),
                  tmaj, tok(d), tok(d), _const_spec((1, d)),
                  _const_spec(wfo.shape), _const_spec(wval.shape), _const_spec(wgate.shape),
                  _const_spec(wout.shape)],
        out_specs=tok(d),
        out_shape=jax.ShapeDtypeStruct((NB, sseg, d), F32),
        scratch_shapes=[pltpu.VMEM((sw // LANES, NB * tt, LANES), F32)],
        compiler_params=pltpu.CompilerParams(
            dimension_semantics=("arbitrary",), vmem_limit_bytes=VMEM_LIMIT),
        name="mix",
    )(xv, f, y8, sa, sb, gpost, wfo, wval, wgate, wout)


def _ffn_kernel(x_ref, gpre_ref, gpost_ref, wg_ref, wu_ref, wd_ref, o_ref, *, tt, d, hidden):
    rows = NB * tt
    x1 = x_ref[...].reshape(rows, d)
    h2 = _rms(x1, gpre_ref[...]).astype(BF16)
    acc = None
    for c0, c1 in _ffn_chunks(hidden):
        g = _dot(h2, wg_ref[:, c0:c1])
        u = _dot(h2, wu_ref[:, c0:c1])
        part = _dot((jax.nn.silu(g) * u).astype(BF16), wd_ref[c0:c1, :])
        acc = part if acc is None else acc + part
    o_ref[...] = (x1 + _rms(acc, gpost_ref[...])).reshape(NB, tt, d)


def _ffn(xv, gpre, gpost, wg, wu, wd, tt):
    nb, sseg, d = xv.shape
    hidden = wg.shape[-1]
    tok = pl.BlockSpec((NB, tt, d), lambda i: (0, i, 0))
    return pl.pallas_call(
        functools.partial(_ffn_kernel, tt=tt, d=d, hidden=hidden),
        grid=(sseg // tt,),
        in_specs=[tok, _const_spec((1, d)), _const_spec((1, d)),
                  _const_spec(wg.shape), _const_spec(wu.shape), _const_spec(wd.shape)],
        out_specs=tok,
        out_shape=jax.ShapeDtypeStruct((NB, sseg, d), F32),
        compiler_params=pltpu.CompilerParams(
            dimension_semantics=("arbitrary",), vmem_limit_bytes=VMEM_LIMIT),
        name="ffn",
    )(xv, gpre, gpost, wg, wu, wd)


def _pick(total, want):
    t = min(want, total)
    assert total % t == 0, (total, t)
    return t


def _encoder_layer(x, p):
    b, s, d = x.shape
    assert b in (1, NB), "one sequence (split in NB segments) or NB sequences"
    sseg = (b * s) // NB
    fw = p["w_fnet_out"].shape[0]
    sw = p["w_glu_val"].shape[0]
    xv = x.reshape(NB, sseg, d)
    tt = _pick(sseg, TOKEN_TILE_T)

    uf, u8, sa, sb = _in_proj(xv, p["norm_mix_pre"], p["w_in"], fw, sw, tt)
    ng = fw // LANES
    f = _fourier_mix(uf.reshape(ng, b, s, LANES)).reshape(ng, NB, sseg, LANES)

    tables = _ssm_tables(p["lam_re"], p["lam_im"], p["log_dt"], p["b_re"], p["b_im"],
                         p["c_re"], p["c_im"], p["d_skip"], sseg)
    ys = _ssm(u8, tables, chained=(b == 1), mc=_pick(sseg // SCAN_BLOCK, SCAN_CHUNK_BLOCKS))

    x1 = _mix(xv, f, ys, sa, sb, p["norm_mix_post"],
              p["w_fnet_out"], p["w_glu_val"], p["w_glu_gate"], p["w_out"], tt)
    y = _ffn(x1, p["norm_ffn_pre"], p["norm_ffn_post"],
             p["w_ffn_gate"], p["w_ffn_up"], p["w_ffn_down"], tt)
    return y.reshape(b, s, d)


def _layer_params(l, norm_mix_pre, norm_mix_post, norm_ffn_pre, norm_ffn_post, w_in, w_fnet_out,
                  lam_re, lam_im, log_dt, b_re, b_im, c_re, c_im, d_skip, w_glu_val, w_glu_gate,
                  w_out, w_ffn_gate, w_ffn_up, w_ffn_down):
    row = lambda v: v[l].astype(F32).reshape(1, -1)
    w = lambda v: v[l].astype(BF16)
    return dict(
        norm_mix_pre=row(norm_mix_pre), norm_mix_post=row(norm_mix_post),
        norm_ffn_pre=row(norm_ffn_pre), norm_ffn_post=row(norm_ffn_post),
        w_in=w(w_in), w_fnet_out=w(w_fnet_out),
        lam_re=lam_re[l], lam_im=lam_im[l], log_dt=log_dt[l],
        b_re=b_re[l], b_im=b_im[l], c_re=c_re[l], c_im=c_im[l], d_skip=row(d_skip),
        w_glu_val=w(w_glu_val), w_glu_gate=w(w_glu_gate), w_out=w(w_out),
        w_ffn_gate=w(w_ffn_gate), w_ffn_up=w(w_ffn_up), w_ffn_down=w(w_ffn_down))


def kernel(x_prompt, x_sample, norm_mix_pre, norm_mix_post, norm_ffn_pre, norm_ffn_post, w_in, w_fnet_out, lam_re, lam_im, log_dt, b_re, b_im, c_re, c_im, d_skip, w_glu_val, w_glu_gate, w_out, w_ffn_gate, w_ffn_up, w_ffn_down):
    weights = (norm_mix_pre, norm_mix_post, norm_ffn_pre, norm_ffn_post, w_in, w_fnet_out,
               lam_re, lam_im, log_dt, b_re, b_im, c_re, c_im, d_skip, w_glu_val, w_glu_gate,
               w_out, w_ffn_gate, w_ffn_up, w_ffn_down)
    depth = w_in.shape[0]
    outs = []
    for x in (x_prompt, x_sample):
        for l in range(depth):
            x = _encoder_layer(x, _layer_params(l, *weights))
        outs.append(x)
    return tuple(outs)
```

```python
import functools
import math

import numpy as np
import jax
import jax.numpy as jnp
from jax import lax
from jax.experimental import pallas as pl
from jax.experimental.pallas import tpu as pltpu

F32 = jnp.float32
BF16 = jnp.bfloat16

EPS = 1e-6
SUBLANES = 8
LANES = 128
NB = SUBLANES

FNET_GROUP_DIM = 128
SSM_GROUP_DIM = 16
GROUPS_PER_BLOCK = LANES // SSM_GROUP_DIM
SCAN_BLOCK = LANES // SSM_GROUP_DIM
PAIR = 2
SCAN_INTERLEAVE = 8
SCAN_CHUNK_BLOCKS = 64

TOKEN_TILE_T = 128

FFT_N2 = 128
FFT_S2_BLOCK = 16
FFT_K1_BLOCK = 8
FFT_VMEM_BUDGET = 40 * 1024 * 1024
FFT_GROUPS_PER_PASS = 2
FFT_ROW_PAD = SUBLANES
VMEM_LIMIT = 56 * 1024 * 1024


def _dot(a, b):
    return jnp.dot(a, b, preferred_element_type=F32)


def _rms(x, g):
    return x * lax.rsqrt(jnp.mean(x * x, axis=-1, keepdims=True) + EPS) * g


def _const_spec(shape):
    nd = len(shape)
    return pl.BlockSpec(shape, lambda *_: (0,) * nd, pipeline_mode=pl.Buffered(1))


def _transpose_pieces(vs):
    n = GROUPS_PER_BLOCK
    rows = vs[0].shape[0]
    piece = lax.broadcasted_iota(jnp.int32, (rows, LANES), 1) // SSM_GROUP_DIM
    cur = list(vs)
    d = 1
    while d < n:
        upper = (piece & d) != 0
        nxt = [None] * n
        for c in range(n):
            if c & d == 0:
                lo, hi = cur[c], cur[c | d]
                nxt[c] = jnp.where(upper, pltpu.roll(hi, d * SSM_GROUP_DIM, 1), lo)
                nxt[c | d] = jnp.where(upper, hi, pltpu.roll(lo, LANES - d * SSM_GROUP_DIM, 1))
        cur = nxt
        d *= 2
    return cur


def _in_proj_kernel(x_ref, g_ref, w_ref, uf_ref, u8_ref, sa_ref, sb_ref, us_scr, *, tt, fw, sw, d):
    rows = NB * tt
    o = fw + sw
    x = x_ref[...].reshape(rows, d)
    h = _rms(x, g_ref[...]).astype(BF16)
    uf = _dot(h, w_ref[:, 0:fw])
    for g in range(fw // LANES):
        uf_ref[g] = uf[:, g * LANES:(g + 1) * LANES].reshape(NB, tt, LANES)
    us = _dot(h, w_ref[:, fw:fw + sw])
    sa_ref[...] = jax.nn.sigmoid(_dot(h, w_ref[:, o:o + d])).reshape(NB, tt, d).astype(BF16)
    sb_ref[...] = jax.nn.sigmoid(_dot(h, w_ref[:, o + d:o + 2 * d])).reshape(NB, tt, d).astype(BF16)
    for cb in range(sw // LANES):
        for b in range(NB):
            us_scr[cb, pl.ds(b, tt, stride=NB), :] = us[b * tt:(b + 1) * tt, cb * LANES:(cb + 1) * LANES]
    for cb in range(sw // LANES):
        vs = [jnp.concatenate([us_scr[cb, pl.ds((SCAN_BLOCK * m + j) * NB, NB), :]
                               for m in range(tt // SCAN_BLOCK)], axis=0)
              for j in range(SCAN_BLOCK)]
        folded = _transpose_pieces(vs)
        for gl in range(GROUPS_PER_BLOCK):
            g = cb * GROUPS_PER_BLOCK + gl
            u8_ref[g // PAIR, :, (g % PAIR) * LANES:(g % PAIR + 1) * LANES] = folded[gl].astype(BF16)


def _in_proj(xv, gain, w_in, fw, sw, tt):
    nb, sseg, d = xv.shape
    n = sseg // tt
    nq = sw // (PAIR * SSM_GROUP_DIM)
    assert tt % SCAN_BLOCK == 0
    kern = functools.partial(_in_proj_kernel, tt=tt, fw=fw, sw=sw, d=d)
    return pl.pallas_call(
        kern,
        grid=(n,),
        in_specs=[
            pl.BlockSpec((NB, tt, d), lambda i: (0, i, 0)),
            _const_spec((1, d)),
            _const_spec(w_in.shape),
        ],
        out_specs=[
            pl.BlockSpec((fw // LANES, NB, tt, LANES), lambda i: (0, 0, i, 0)),
            pl.BlockSpec((nq, tt, PAIR * LANES), lambda i: (0, i, 0)),
            pl.BlockSpec((NB, tt, d), lambda i: (0, i, 0)),
            pl.BlockSpec((NB, tt, d), lambda i: (0, i, 0)),
        ],
        out_shape=[
            jax.ShapeDtypeStruct((fw // LANES, NB, sseg, LANES), F32),
            jax.ShapeDtypeStruct((nq, sseg, PAIR * LANES), BF16),
            jax.ShapeDtypeStruct((NB, sseg, d), BF16),
            jax.ShapeDtypeStruct((NB, sseg, d), BF16),
        ],
        scratch_shapes=[pltpu.VMEM((sw // LANES, NB * tt, LANES), F32)],
        compiler_params=pltpu.CompilerParams(
            dimension_semantics=("arbitrary",), vmem_limit_bytes=VMEM_LIMIT),
        name="in_proj",
    )(xv, gain, w_in)


def _fft_tables(s):
    n2 = FFT_N2
    n1 = s // n2
    s1 = np.arange(n1, dtype=np.int64)
    k1 = np.arange(n1, dtype=np.int64)
    s2 = np.arange(n2, dtype=np.int64)
    idx = (k1[None, :, None] * s1[None, None, :] * n2 + s2[:, None, None] * k1[None, :, None]) % s
    ang = 2.0 * np.pi * idx.astype(np.float64) / s
    g = np.stack([np.cos(ang), -np.sin(ang)], axis=1) / math.sqrt(n1)
    g = g.reshape(n2, 2 * n1, n1)
    k2 = np.arange(n2, dtype=np.int64)
    ang2 = 2.0 * np.pi * ((k2[:, None] * s2[None, :]) % n2).astype(np.float64) / n2
    c2, s2m = np.cos(ang2), np.sin(ang2)
    h = np.block([[c2, s2m], [-s2m, c2]]) / math.sqrt(n2)
    c = np.arange(FNET_GROUP_DIM, dtype=np.int64)
    angc = 2.0 * np.pi * ((c[:, None] * c[None, :]) % FNET_GROUP_DIM).astype(np.float64) / FNET_GROUP_DIM
    cs = np.concatenate([np.cos(angc), np.sin(angc)], axis=0) / math.sqrt(FNET_GROUP_DIM)
    return g.astype(np.float32), h.astype(np.float32), cs.astype(np.float32)


def _fft_kernel(x_ref, g_ref, h_ref, cs_ref, o_ref, xs_scr, a_scr, os_scr,
                *, n1, n2, ng, p1, s2b, k1b):
    p = pl.program_id(2)
    kstride = 2 * n2 + FFT_ROW_PAD

    @pl.when(p < p1)
    def _stage1():
        xs_scr[...] = x_ref[...].reshape(ng, n1 * s2b, LANES)
        for j in range(s2b):
            xj = jnp.concatenate(
                [xs_scr[g, pl.ds(j, n1, stride=s2b), :] for g in range(ng)], axis=1).astype(BF16)
            r = _dot(g_ref[j], xj)
            s2 = p * s2b + j
            for g in range(ng):
                a_scr[g, pl.ds(s2, n1, stride=kstride), :] = r[0:n1, g * LANES:(g + 1) * LANES]
                a_scr[g, pl.ds(n2 + s2, n1, stride=kstride), :] = r[n1:2 * n1, g * LANES:(g + 1) * LANES]

    @pl.when(p >= p1)
    def _stage2():
        for j in range(k1b):
            row0 = pl.multiple_of(((p - p1) * k1b + j) * kstride, SUBLANES)
            a = jnp.concatenate(
                [a_scr[g, pl.ds(row0, 2 * n2), :] for g in range(ng)], axis=1).astype(BF16)
            y = _dot(h_ref[...], a).astype(BF16)
            for g in range(ng):
                yg = jnp.concatenate(
                    [y[0:n2, g * LANES:(g + 1) * LANES], y[n2:2 * n2, g * LANES:(g + 1) * LANES]], axis=1)
                os_scr[g, pl.ds(j, n2, stride=k1b), :] = _dot(yg, cs_ref[...])
        o_ref[...] = os_scr[...].reshape(ng, n2, k1b, LANES)


def _fourier_mix(u):
    ngroups, b, s, c = u.shape
    assert c == FNET_GROUP_DIM == LANES
    n2 = FFT_N2
    n1 = s // n2
    ng = FFT_GROUPS_PER_PASS
    kstride = 2 * n2 + FFT_ROW_PAD

    def vmem_bytes(s2b, k1b):
        f32 = 4 * ng * c
        return (f32 * n1 * kstride + 3 * f32 * n1 * s2b + 3 * f32 * n2 * k1b
                + 2 * 2 * s2b * 2 * n1 * max(n1, LANES))

    s2b, k1b = FFT_S2_BLOCK, FFT_K1_BLOCK
    while (vmem_bytes(2 * s2b, 2 * k1b) <= FFT_VMEM_BUDGET
           and n2 % (2 * s2b) == 0 and n1 % (2 * k1b) == 0):
        s2b, k1b = 2 * s2b, 2 * k1b
    assert n1 * n2 == s and n1 % k1b == 0 and n2 % s2b == 0 and ngroups % ng == 0
    p1, p2 = n2 // s2b, n1 // k1b
    g_np, h_np, cs_np = _fft_tables(s)
    g_t = jnp.asarray(g_np).astype(BF16)
    h_t = jnp.asarray(h_np).astype(BF16)
    cs_t = jnp.asarray(cs_np).astype(BF16)

    f = pl.pallas_call(
        functools.partial(_fft_kernel, n1=n1, n2=n2, ng=ng, p1=p1, s2b=s2b, k1b=k1b),
        grid=(b, ngroups // ng, p1 + p2),
        in_specs=[
            pl.BlockSpec((ng, None, n1, s2b, c),
                         lambda bi, gi, p: (gi, bi, 0, jnp.minimum(p, p1 - 1), 0)),
            pl.BlockSpec((s2b, 2 * n1, n1), lambda bi, gi, p: (jnp.minimum(p, p1 - 1), 0, 0)),
            _const_spec((2 * n2, 2 * n2)),
            _const_spec((2 * FNET_GROUP_DIM, FNET_GROUP_DIM)),
        ],
        out_specs=pl.BlockSpec((ng, None, n2, k1b, c),
                               lambda bi, gi, p: (gi, bi, 0, jnp.maximum(p - p1, 0), 0)),
        out_shape=jax.ShapeDtypeStruct((ngroups, b, n2, n1, c), F32),
        scratch_shapes=[pltpu.VMEM((ng, n1 * s2b, c), F32),
                        pltpu.VMEM((ng, n1 * kstride, c), F32),
                        pltpu.VMEM((ng, n2 * k1b, c), F32)],
        compiler_params=pltpu.CompilerParams(
            dimension_semantics=("arbitrary", "arbitrary", "arbitrary"),
            vmem_limit_bytes=VMEM_LIMIT),
        name="fft",
    )(u.reshape(ngroups, b, n1, n2, c), g_t, h_t, cs_t)
    return f.reshape(ngroups, b, s, c)


def _ssm_tables(lam_re, lam_im, log_dt, b_re, b_im, c_re, c_im, d_skip, seg_len):
    assert PAIR == 2
    ndir, ng, p = lam_re.shape
    hd = b_re.shape[-1]
    r = SCAN_BLOCK
    nq = ng // PAIR
    lam_re = lam_re.astype(F32)
    lam_im = lam_im.astype(F32)
    dt = jnp.exp(log_dt.astype(F32))[..., None]

    def apow(k):
        mag = jnp.exp(lam_re * dt * k)
        return mag * jnp.cos(lam_im * dt * k), mag * jnp.sin(lam_im * dt * k)

    ar, ai = apow(1.0)
    den = lam_re * lam_re + lam_im * lam_im
    qr = ((ar - 1.0) * lam_re + ai * lam_im) / den
    qi = (ai * lam_re - (ar - 1.0) * lam_im) / den
    bre = qr[..., None] * b_re.astype(F32) - qi[..., None] * b_im.astype(F32)
    bim = qr[..., None] * b_im.astype(F32) + qi[..., None] * b_re.astype(F32)
    cr = c_re.astype(F32)
    ci = c_im.astype(F32)

    ks = jnp.arange(r + 1, dtype=F32).reshape(r + 1, 1, 1, 1)
    pr, pi = apow(ks)
    abr = pr[:r, ..., None] * bre - pi[:r, ..., None] * bim
    abi = pr[:r, ..., None] * bim + pi[:r, ..., None] * bre

    def in_rows(x):
        return jnp.transpose(x, (1, 0, 3, 2)).reshape(ng, r * hd, p)

    bp_re = jnp.stack([in_rows(abr[::-1, 0]), in_rows(abr[:, 1])])
    bp_im = jnp.stack([in_rows(abi[::-1, 0]), in_rows(abi[:, 1])])

    klag = jnp.sum(cr[None, :, :, :, :, None] * abr[:, :, :, None, :, :]
                   - ci[None, :, :, :, :, None] * abi[:, :, :, None, :, :], axis=4)
    kzero = jnp.zeros_like(klag[0, 0])
    kin = jnp.stack([
        jnp.stack([(klag[i - j, 0] if i >= j else kzero) + (klag[j - i, 1] if j >= i else kzero)
                   for i in range(r)]) for j in range(r)])
    kin = jnp.transpose(kin, (2, 0, 4, 1, 3)).reshape(ng, r * hd, r * hd)
    dsk = d_skip.astype(F32).reshape(ng, hd)
    kin = kin + jnp.eye(r * hd, dtype=F32)[None] * jnp.tile(dsk, (1, r))[:, None, :]

    def state_out(d, pows):
        er = cr[d][None] * pr[pows, d][:, :, None, :] - ci[d][None] * pi[pows, d][:, :, None, :]
        ei = cr[d][None] * pi[pows, d][:, :, None, :] + ci[d][None] * pr[pows, d][:, :, None, :]
        to_rows = lambda e: jnp.transpose(e, (1, 3, 0, 2)).reshape(ng, p, r * hd)
        return to_rows(er), -to_rows(ei)

    cf_re, cf_im = state_out(0, jnp.arange(1, r + 1))
    cb_re, cb_im = state_out(1, r - jnp.arange(r))

    def pair_bd(m):
        lead = m.shape[:-3]
        rr, cc = m.shape[-2:]
        m = m.reshape(lead + (nq, PAIR, rr, cc))
        m0, m1 = m[..., 0, :, :], m[..., 1, :, :]
        z = jnp.zeros_like(m0)
        return jnp.concatenate([jnp.concatenate([m0, z], axis=-1),
                                jnp.concatenate([z, m1], axis=-1)], axis=-2)

    bq = jnp.concatenate([pair_bd(bp_re), pair_bd(bp_im)], axis=-1).astype(BF16)
    wy = jnp.concatenate([pair_bd(kin), pair_bd(cf_re), pair_bd(cf_im),
                          pair_bd(cb_re), pair_bd(cb_im)], axis=-2).astype(BF16)

    def lanes(v):
        v = v.reshape(ndir, nq, 1, PAIR * p)
        return jnp.broadcast_to(v, (ndir, nq, SUBLANES, PAIR * p))

    a8r, a8i = apow(float(r))
    alr, ali = apow(float(seg_len))
    return bq, wy, lanes(a8r), lanes(a8i), lanes(alr), lanes(ali)


def _scan_kernel(*refs, mc, nq, reverse, chain, emit):
    u8_ref, e_ref, bq_ref, a8re_ref, a8im_ref, alre_ref, alim_ref = refs[:7]
    if emit == "y":
        xo_ref, wy_ref, y8_ref, eo_ref, d_scr, st_scr = refs[7:]
    elif emit == "x":
        x_ref, eo_ref, d_scr, st_scr = refs[7:]
    else:
        eo_ref, d_scr, st_scr = refs[7:]
    i = pl.program_id(0)
    n = pl.num_programs(0)
    rows = mc * SUBLANES
    hl = LANES

    @pl.when(i == 0)
    def _init():
        if not chain:
            st_scr[...] = jnp.zeros(st_scr.shape, F32)
        else:
            row = lax.broadcasted_iota(jnp.int32, (SUBLANES, hl), 0)
            shift = SUBLANES - 1 if reverse else 1
            keep = (row <= SUBLANES - 2) if reverse else (row >= 1)
            for q in range(nq):
                er = pltpu.roll(e_ref[q, :, 0:hl], shift, 0)
                ei = pltpu.roll(e_ref[q, :, hl:2 * hl], shift, 0)
                lr = alre_ref[q]
                li = alim_ref[q]
                xr = jnp.zeros((SUBLANES, hl), F32)
                xi = jnp.zeros((SUBLANES, hl), F32)
                for _ in range(SUBLANES - 1):
                    pr = pltpu.roll(xr, shift, 0)
                    pi = pltpu.roll(xi, shift, 0)
                    xr = jnp.where(keep, lr * pr - li * pi + er, 0.0)
                    xi = jnp.where(keep, lr * pi + li * pr + ei, 0.0)
                st_scr[q, :, 0:hl] = xr
                st_scr[q, :, hl:2 * hl] = xi

    for q0 in range(0, nq, SCAN_INTERLEAVE):
        qs = list(range(q0, min(q0 + SCAN_INTERLEAVE, nq)))
        for k, q in enumerate(qs):
            d_scr[k] = _dot(u8_ref[q], bq_ref[q])
        a_re = [a8re_ref[q] for q in qs]
        a_im = [a8im_ref[q] for q in qs]

        def body(step, carry, a_re=a_re, a_im=a_im, nk=len(qs)):
            m = (mc - 1 - step) if reverse else step
            r = pl.multiple_of(m * SUBLANES, SUBLANES)
            out = []
            for k in range(nk):
                re, im = carry[2 * k], carry[2 * k + 1]
                d_re = d_scr[k, pl.ds(r, SUBLANES), 0:hl]
                d_im = d_scr[k, pl.ds(r, SUBLANES), hl:2 * hl]
                d_scr[k, pl.ds(r, SUBLANES), 0:hl] = re
                d_scr[k, pl.ds(r, SUBLANES), hl:2 * hl] = im
                out.append(a_re[k] * re - a_im[k] * im + d_re)
                out.append(a_re[k] * im + a_im[k] * re + d_im)
            return tuple(out)

        init = []
        for q in qs:
            init += [st_scr[q, :, 0:hl], st_scr[q, :, hl:2 * hl]]
        fin = lax.fori_loop(0, mc, body, tuple(init), unroll=2)
        for k, q in enumerate(qs):
            st_scr[q, :, 0:hl] = fin[2 * k]
            st_scr[q, :, hl:2 * hl] = fin[2 * k + 1]
            if emit == "x":
                x_ref[q] = d_scr[k].astype(BF16)
            elif emit == "y":
                lhs = jnp.concatenate([u8_ref[q], xo_ref[q], d_scr[k].astype(BF16)], axis=1)
                y8_ref[q] = _dot(lhs, wy_ref[q]).astype(BF16)

    @pl.when(i == n - 1)
    def _fin():
        eo_ref[...] = st_scr[...]


def _scan(u8, e_in, bq, a8re, a8im, alre, alim, mc, reverse, chain, emit, x_other=None, wy=None):
    nq = bq.shape[0]
    mtot = u8.shape[1] // NB
    n = mtot // mc
    rows = mc * NB
    order = (lambda i: n - 1 - i) if reverse else (lambda i: i)
    st_shape = (nq, SUBLANES, 2 * LANES)
    row_spec = pl.BlockSpec((nq, rows, 2 * LANES), lambda i: (0, order(i), 0))
    row_shape = jax.ShapeDtypeStruct((nq, mtot * NB, 2 * LANES), BF16)
    in_specs = [row_spec, _const_spec(st_shape), _const_spec(bq.shape),
                _const_spec(a8re.shape), _const_spec(a8im.shape),
                _const_spec(alre.shape), _const_spec(alim.shape)]
    args = [u8, e_in, bq, a8re, a8im, alre, alim]
    out_specs = [_const_spec(st_shape)]
    out_shape = [jax.ShapeDtypeStruct(st_shape, F32)]
    if emit == "y":
        in_specs += [row_spec, _const_spec(wy.shape)]
        args += [x_other, wy]
    if emit is not None:
        out_specs = [row_spec] + out_specs
        out_shape = [row_shape] + out_shape
    kern = functools.partial(_scan_kernel, mc=mc, nq=nq, reverse=reverse, chain=chain, emit=emit)
    return pl.pallas_call(
        kern,
        grid=(n,),
        in_specs=in_specs,
        out_specs=out_specs,
        out_shape=out_shape,
        scratch_shapes=[pltpu.VMEM((SCAN_INTERLEAVE, rows, 2 * LANES), F32),
                        pltpu.VMEM(st_shape, F32)],
        compiler_params=pltpu.CompilerParams(
            dimension_semantics=("arbitrary",), vmem_limit_bytes=VMEM_LIMIT),
        name="ssm_scan_" + ("bwd" if reverse else "fwd") + ("_chain" if chain else "")
             + ("_" + emit if emit else "_end"),
    )(*args)


def _ssm(u8, tables, chained, mc):
    bq, wy, a8re, a8im, alre, alim = tables
    nq = bq.shape[1]
    ef = eb = jnp.zeros((nq, SUBLANES, 2 * LANES), F32)
    par = lambda dr: (bq[dr], a8re[dr], a8im[dr], alre[dr], alim[dr])
    if chained:
        (ef,) = _scan(u8, ef, *par(0), mc, False, False, None)
        (eb,) = _scan(u8, eb, *par(1), mc, True, False, None)
    xf, _ = _scan(u8, ef, *par(0), mc, False, chained, "x")
    y8, _ = _scan(u8, eb, *par(1), mc, True, chained, "y", xf, wy)
    return y8


def _ffn_chunks(hidden, step=1024):
    edges = list(range(0, hidden, step)) + [hidden]
    return list(zip(edges[:-1], edges[1:]))


def _mix_kernel(x_ref, f_ref, y8_ref, sa_ref, sb_ref, gpost_ref,
                wfo_ref, wval_ref, wgate_ref, wout_ref, o_ref, y_scr, *, tt, d, fw, sw):
    rows = NB * tt
    qpb = GROUPS_PER_BLOCK // PAIR
    for cb in range(sw // LANES):
        vs = [y8_ref[cb * qpb + gl // PAIR, :, (gl % PAIR) * LANES:(gl % PAIR + 1) * LANES].astype(F32)
              for gl in range(GROUPS_PER_BLOCK)]
        unfolded = _transpose_pieces(vs)
        for i in range(SCAN_BLOCK):
            for m in range(tt // SCAN_BLOCK):
                y_scr[cb, pl.ds((SCAN_BLOCK * m + i) * NB, NB), :] = unfolded[i][m * NB:(m + 1) * NB, :]
    ys = jnp.concatenate(
        [jnp.concatenate(
            [y_scr[cb, pl.ds(b, tt, stride=NB), :] for b in range(NB)], axis=0)
         for cb in range(sw // LANES)], axis=1)
    z = jax.nn.gelu(ys).astype(BF16)
    f = jnp.concatenate(
        [f_ref[g].reshape(rows, LANES) for g in range(fw // LANES)], axis=1).astype(BF16)
    br_a = _dot(f, wfo_ref[...])
    br_b = _dot(z, wval_ref[...]) * jax.nn.sigmoid(_dot(z, wgate_ref[...]))
    sa = sa_ref[...].reshape(rows, d).astype(F32)
    sb = sb_ref[...].reshape(rows, d).astype(F32)
    merged = (sa * br_a + sb * br_b).astype(BF16)
    m = _dot(merged, wout_ref[...])
    o_ref[...] = (x_ref[...].reshape(rows, d) + _rms(m, gpost_ref[...])).reshape(NB, tt, d)


def _mix(xv, f, y8, sa, sb, gpost, wfo, wval, wgate, wout, tt):
    nb, sseg, d = xv.shape
    fw = f.shape[0] * LANES
    nq = y8.shape[0]
    sw = nq * PAIR * SSM_GROUP_DIM
    n = sseg // tt
    assert tt % SCAN_BLOCK == 0
    kern = functools.partial(_mix_kernel, tt=tt, d=d, fw=fw, sw=sw)
    tok = lambda width: pl.BlockSpec((NB, tt, width), lambda i: (0, i, 0))
    tmaj = pl.BlockSpec((nq, tt, PAIR * LANES), lambda i: (0, i, 0))
    return pl.pallas_call(
        kern,
        grid=(n,),
        in_specs=[tok(d), pl.BlockSpec((fw // LANES, NB, tt, LANES), lambda i: (0, 0, i, 0)),
                  tmaj, tok(d), tok(d), _const_spec((1, d)),
                  _const_spec(wfo.shape), _const_spec(wval.shape), _const_spec(wgate.shape),
                  _const_spec(wout.shape)],
        out_specs=tok(d),
        out_shape=jax.ShapeDtypeStruct((NB, sseg, d), F32),
        scratch_shapes=[pltpu.VMEM((sw // LANES, NB * tt, LANES), F32)],
        compiler_params=pltpu.CompilerParams(
            dimension_semantics=("arbitrary",), vmem_limit_bytes=VMEM_LIMIT),
        name="mix",
    )(xv, f, y8, sa, sb, gpost, wfo, wval, wgate, wout)


def _ffn_kernel(x_ref, gpre_ref, gpost_ref, wg_ref, wu_ref, wd_ref, o_ref, *, tt, d, hidden):
    rows = NB * tt
    x1 = x_ref[...].reshape(rows, d)
    h2 = _rms(x1, gpre_ref[...]).astype(BF16)
    acc = None
    for c0, c1 in _ffn_chunks(hidden):
        g = _dot(h2, wg_ref[:, c0:c1])
        u = _dot(h2, wu_ref[:, c0:c1])
        part = _dot((jax.nn.silu(g) * u).astype(BF16), wd_ref[c0:c1, :])
        acc = part if acc is None else acc + part
    o_ref[...] = (x1 + _rms(acc, gpost_ref[...])).reshape(NB, tt, d)


def _ffn(xv, gpre, gpost, wg, wu, wd, tt):
    nb, sseg, d = xv.shape
    hidden = wg.shape[-1]
    tok = pl.BlockSpec((NB, tt, d), lambda i: (0, i, 0))
    return pl.pallas_call(
        functools.partial(_ffn_kernel, tt=tt, d=d, hidden=hidden),
        grid=(sseg // tt,),
        in_specs=[tok, _const_spec((1, d)), _const_spec((1, d)),
                  _const_spec(wg.shape), _const_spec(wu.shape), _const_spec(wd.shape)],
        out_specs=tok,
        out_shape=jax.ShapeDtypeStruct((NB, sseg, d), F32),
        compiler_params=pltpu.CompilerParams(
            dimension_semantics=("arbitrary",), vmem_limit_bytes=VMEM_LIMIT),
        name="ffn",
    )(xv, gpre, gpost, wg, wu, wd)


def _pick(total, want):
    t = min(want, total)
    assert total % t == 0, (total, t)
    return t


def _encoder_layer(x, p):
    b, s, d = x.shape
    assert b in (1, NB), "one sequence (split in NB segments) or NB sequences"
    sseg = (b * s) // NB
    fw = p["w_fnet_out"].shape[0]
    sw = p["w_glu_val"].shape[0]
    xv = x.reshape(NB, sseg, d)
    tt = _pick(sseg, TOKEN_TILE_T)

    uf, u8, sa, sb = _in_proj(xv, p["norm_mix_pre"], p["w_in"], fw, sw, tt)
    ng = fw // LANES
    f = _fourier_mix(uf.reshape(ng, b, s, LANES)).reshape(ng, NB, sseg, LANES)

    tables = _ssm_tables(p["lam_re"], p["lam_im"], p["log_dt"], p["b_re"], p["b_im"],
                         p["c_re"], p["c_im"], p["d_skip"], sseg)
    ys = _ssm(u8, tables, chained=(b == 1), mc=_pick(sseg // SCAN_BLOCK, SCAN_CHUNK_BLOCKS))

    x1 = _mix(xv, f, ys, sa, sb, p["norm_mix_post"],
              p["w_fnet_out"], p["w_glu_val"], p["w_glu_gate"], p["w_out"], tt)
    y = _ffn(x1, p["norm_ffn_pre"], p["norm_ffn_post"],
             p["w_ffn_gate"], p["w_ffn_up"], p["w_ffn_down"], tt)
    return y.reshape(b, s, d)


def _layer_params(l, norm_mix_pre, norm_mix_post, norm_ffn_pre, norm_ffn_post, w_in, w_fnet_out,
                  lam_re, lam_im, log_dt, b_re, b_im, c_re, c_im, d_skip, w_glu_val, w_glu_gate,
                  w_out, w_ffn_gate, w_ffn_up, w_ffn_down):
    row = lambda v: v[l].astype(F32).reshape(1, -1)
    w = lambda v: v[l].astype(BF16)
    return dict(
        norm_mix_pre=row(norm_mix_pre), norm_mix_post=row(norm_mix_post),
        norm_ffn_pre=row(norm_ffn_pre), norm_ffn_post=row(norm_ffn_post),
        w_in=w(w_in), w_fnet_out=w(w_fnet_out),
        lam_re=lam_re[l], lam_im=lam_im[l], log_dt=log_dt[l],
        b_re=b_re[l], b_im=b_im[l], c_re=c_re[l], c_im=c_im[l], d_skip=row(d_skip),
        w_glu_val=w(w_glu_val), w_glu_gate=w(w_glu_gate), w_out=w(w_out),
        w_ffn_gate=w(w_ffn_gate), w_ffn_up=w(w_ffn_up), w_ffn_down=w(w_ffn_down))


def kernel(x_prompt, x_sample, norm_mix_pre, norm_mix_post, norm_ffn_pre, norm_ffn_post, w_in, w_fnet_out, lam_re, lam_im, log_dt, b_re, b_im, c_re, c_im, d_skip, w_glu_val, w_glu_gate, w_out, w_ffn_gate, w_ffn_up, w_ffn_down):
    weights = (norm_mix_pre, norm_mix_post, norm_ffn_pre, norm_ffn_post, w_in, w_fnet_out,
               lam_re, lam_im, log_dt, b_re, b_im, c_re, c_im, d_skip, w_glu_val, w_glu_gate,
               w_out, w_ffn_gate, w_ffn_up, w_ffn_down)
    depth = w_in.shape[0]
    outs = []
    for x in (x_prompt, x_sample):
        for l in range(depth):
            x = _encoder_layer(x, _layer_params(l, *weights))
        outs.append(x)
    return tuple(outs)
```

```python
import functools
import math

import numpy as np
import jax
import jax.numpy as jnp
from jax import lax
from jax.experimental import pallas as pl
from jax.experimental.pallas import tpu as pltpu

F32 = jnp.float32
BF16 = jnp.bfloat16

EPS = 1e-6
SUBLANES = 8
LANES = 128
NB = SUBLANES

FNET_GROUP_DIM = 128
SSM_GROUP_DIM = 16
GROUPS_PER_BLOCK = LANES // SSM_GROUP_DIM
SCAN_BLOCK = LANES // SSM_GROUP_DIM
PAIR = 2
SCAN_INTERLEAVE = 8
SCAN_CHUNK_BLOCKS = 64

TOKEN_TILE_T = 128

FFT_N2 = 128
FFT_S2_BLOCK = 16
FFT_K1_BLOCK = 8
FFT_VMEM_BUDGET = 40 * 1024 * 1024
FFT_GROUPS_PER_PASS = 2
FFT_ROW_PAD = SUBLANES
VMEM_LIMIT = 56 * 1024 * 1024


def _dot(a, b):
    return jnp.dot(a, b, preferred_element_type=F32)


def _sigmoid(x):
    return 0.5 * jnp.tanh(0.5 * x) + 0.5


def _rms(x, g):
    return x * lax.rsqrt(jnp.mean(x * x, axis=-1, keepdims=True) + EPS) * g


def _const_spec(shape):
    nd = len(shape)
    return pl.BlockSpec(shape, lambda *_: (0,) * nd, pipeline_mode=pl.Buffered(1))


def _transpose_pieces(vs):
    n = GROUPS_PER_BLOCK
    rows = vs[0].shape[0]
    piece = lax.broadcasted_iota(jnp.int32, (rows, LANES), 1) // SSM_GROUP_DIM
    cur = list(vs)
    d = 1
    while d < n:
        upper = (piece & d) != 0
        nxt = [None] * n
        for c in range(n):
            if c & d == 0:
                lo, hi = cur[c], cur[c | d]
                nxt[c] = jnp.where(upper, pltpu.roll(hi, d * SSM_GROUP_DIM, 1), lo)
                nxt[c | d] = jnp.where(upper, hi, pltpu.roll(lo, LANES - d * SSM_GROUP_DIM, 1))
        cur = nxt
        d *= 2
    return cur


def _in_proj_kernel(x_ref, g_ref, w_ref, uf_ref, u8_ref, sa_ref, sb_ref, us_scr, *, tt, fw, sw, d):
    rows = NB * tt
    o = fw + sw
    x = x_ref[...].reshape(rows, d)
    h = _rms(x, g_ref[...]).astype(BF16)
    uf = _dot(h, w_ref[:, 0:fw])
    for g in range(fw // LANES):
        uf_ref[g] = uf[:, g * LANES:(g + 1) * LANES].reshape(NB, tt, LANES)
    us = _dot(h, w_ref[:, fw:fw + sw])
    sa_ref[...] = _sigmoid(_dot(h, w_ref[:, o:o + d])).reshape(NB, tt, d).astype(BF16)
    sb_ref[...] = _sigmoid(_dot(h, w_ref[:, o + d:o + 2 * d])).reshape(NB, tt, d).astype(BF16)
    for cb in range(sw // LANES):
        for b in range(NB):
            us_scr[cb, pl.ds(b, tt, stride=NB), :] = us[b * tt:(b + 1) * tt, cb * LANES:(cb + 1) * LANES]
    for cb in range(sw // LANES):
        vs = [jnp.concatenate([us_scr[cb, pl.ds((SCAN_BLOCK * m + j) * NB, NB), :]
                               for m in range(tt // SCAN_BLOCK)], axis=0)
              for j in range(SCAN_BLOCK)]
        folded = _transpose_pieces(vs)
        for gl in range(GROUPS_PER_BLOCK):
            g = cb * GROUPS_PER_BLOCK + gl
            u8_ref[g // PAIR, :, (g % PAIR) * LANES:(g % PAIR + 1) * LANES] = folded[gl].astype(BF16)


def _in_proj(xv, gain, w_in, fw, sw, tt):
    nb, sseg, d = xv.shape
    n = sseg // tt
    nq = sw // (PAIR * SSM_GROUP_DIM)
    assert tt % SCAN_BLOCK == 0
    kern = functools.partial(_in_proj_kernel, tt=tt, fw=fw, sw=sw, d=d)
    return pl.pallas_call(
        kern,
        grid=(n,),
        in_specs=[
            pl.BlockSpec((NB, tt, d), lambda i: (0, i, 0)),
            _const_spec((1, d)),
            _const_spec(w_in.shape),
        ],
        out_specs=[
            pl.BlockSpec((fw // LANES, NB, tt, LANES), lambda i: (0, 0, i, 0)),
            pl.BlockSpec((nq, tt, PAIR * LANES), lambda i: (0, i, 0)),
            pl.BlockSpec((NB, tt, d), lambda i: (0, i, 0)),
            pl.BlockSpec((NB, tt, d), lambda i: (0, i, 0)),
        ],
        out_shape=[
            jax.ShapeDtypeStruct((fw // LANES, NB, sseg, LANES), F32),
            jax.ShapeDtypeStruct((nq, sseg, PAIR * LANES), BF16),
            jax.ShapeDtypeStruct((NB, sseg, d), BF16),
            jax.ShapeDtypeStruct((NB, sseg, d), BF16),
        ],
        scratch_shapes=[pltpu.VMEM((sw // LANES, NB * tt, LANES), F32)],
        compiler_params=pltpu.CompilerParams(
            dimension_semantics=("arbitrary",), vmem_limit_bytes=VMEM_LIMIT),
        name="in_proj",
    )(xv, gain, w_in)


def _fft_tables(s):
    n2 = FFT_N2
    n1 = s // n2
    s1 = np.arange(n1, dtype=np.int64)
    k1 = np.arange(n1, dtype=np.int64)
    s2 = np.arange(n2, dtype=np.int64)
    idx = (k1[None, :, None] * s1[None, None, :] * n2 + s2[:, None, None] * k1[None, :, None]) % s
    ang = 2.0 * np.pi * idx.astype(np.float64) / s
    g = np.stack([np.cos(ang), -np.sin(ang)], axis=1) / math.sqrt(n1)
    g = g.reshape(n2, 2 * n1, n1)
    k2 = np.arange(n2, dtype=np.int64)
    ang2 = 2.0 * np.pi * ((k2[:, None] * s2[None, :]) % n2).astype(np.float64) / n2
    c2, s2m = np.cos(ang2), np.sin(ang2)
    h = np.block([[c2, s2m], [-s2m, c2]]) / math.sqrt(n2)
    c = np.arange(FNET_GROUP_DIM, dtype=np.int64)
    angc = 2.0 * np.pi * ((c[:, None] * c[None, :]) % FNET_GROUP_DIM).astype(np.float64) / FNET_GROUP_DIM
    cs = np.concatenate([np.cos(angc), np.sin(angc)], axis=0) / math.sqrt(FNET_GROUP_DIM)
    return g.astype(np.float32), h.astype(np.float32), cs.astype(np.float32)


def _fft_kernel(x_ref, g_ref, h_ref, cs_ref, o_ref, xs_scr, a_scr, os_scr,
                *, n1, n2, ng, p1, s2b, k1b):
    p = pl.program_id(2)
    kstride = 2 * n2 + FFT_ROW_PAD

    @pl.when(p < p1)
    def _stage1():
        xs_scr[...] = x_ref[...].reshape(ng, n1 * s2b, LANES)
        for j in range(s2b):
            xj = jnp.concatenate(
                [xs_scr[g, pl.ds(j, n1, stride=s2b), :] for g in range(ng)], axis=1).astype(BF16)
            r = _dot(g_ref[j], xj)
            s2 = p * s2b + j
            for g in range(ng):
                a_scr[g, pl.ds(s2, n1, stride=kstride), :] = r[0:n1, g * LANES:(g + 1) * LANES]
                a_scr[g, pl.ds(n2 + s2, n1, stride=kstride), :] = r[n1:2 * n1, g * LANES:(g + 1) * LANES]

    @pl.when(p >= p1)
    def _stage2():
        for j in range(k1b):
            row0 = pl.multiple_of(((p - p1) * k1b + j) * kstride, SUBLANES)
            a = jnp.concatenate(
                [a_scr[g, pl.ds(row0, 2 * n2), :] for g in range(ng)], axis=1).astype(BF16)
            y = _dot(h_ref[...], a).astype(BF16)
            for g in range(ng):
                yg = jnp.concatenate(
                    [y[0:n2, g * LANES:(g + 1) * LANES], y[n2:2 * n2, g * LANES:(g + 1) * LANES]], axis=1)
                os_scr[g, pl.ds(j, n2, stride=k1b), :] = _dot(yg, cs_ref[...])
        o_ref[...] = os_scr[...].reshape(ng, n2, k1b, LANES)


def _fourier_mix(u):
    ngroups, b, s, c = u.shape
    assert c == FNET_GROUP_DIM == LANES
    n2 = FFT_N2
    n1 = s // n2
    ng = FFT_GROUPS_PER_PASS
    kstride = 2 * n2 + FFT_ROW_PAD

    def vmem_bytes(s2b, k1b):
        f32 = 4 * ng * c
        return (f32 * n1 * kstride + 3 * f32 * n1 * s2b + 3 * f32 * n2 * k1b
                + 2 * 2 * s2b * 2 * n1 * max(n1, LANES))

    s2b, k1b = FFT_S2_BLOCK, FFT_K1_BLOCK
    while (vmem_bytes(2 * s2b, 2 * k1b) <= FFT_VMEM_BUDGET
           and n2 % (2 * s2b) == 0 and n1 % (2 * k1b) == 0):
        s2b, k1b = 2 * s2b, 2 * k1b
    assert n1 * n2 == s and n1 % k1b == 0 and n2 % s2b == 0 and ngroups % ng == 0
    p1, p2 = n2 // s2b, n1 // k1b
    g_np, h_np, cs_np = _fft_tables(s)
    g_t = jnp.asarray(g_np).astype(BF16)
    h_t = jnp.asarray(h_np).astype(BF16)
    cs_t = jnp.asarray(cs_np).astype(BF16)

    f = pl.pallas_call(
        functools.partial(_fft_kernel, n1=n1, n2=n2, ng=ng, p1=p1, s2b=s2b, k1b=k1b),
        grid=(b, ngroups // ng, p1 + p2),
        in_specs=[
            pl.BlockSpec((ng, None, n1, s2b, c),
                         lambda bi, gi, p: (gi, bi, 0, jnp.minimum(p, p1 - 1), 0)),
            pl.BlockSpec((s2b, 2 * n1, n1), lambda bi, gi, p: (jnp.minimum(p, p1 - 1), 0, 0)),
            _const_spec((2 * n2, 2 * n2)),
            _const_spec((2 * FNET_GROUP_DIM, FNET_GROUP_DIM)),
        ],
        out_specs=pl.BlockSpec((ng, None, n2, k1b, c),
                               lambda bi, gi, p: (gi, bi, 0, jnp.maximum(p - p1, 0), 0)),
        out_shape=jax.ShapeDtypeStruct((ngroups, b, n2, n1, c), F32),
        scratch_shapes=[pltpu.VMEM((ng, n1 * s2b, c), F32),
                        pltpu.VMEM((ng, n1 * kstride, c), F32),
                        pltpu.VMEM((ng, n2 * k1b, c), F32)],
        compiler_params=pltpu.CompilerParams(
            dimension_semantics=("arbitrary", "arbitrary", "arbitrary"),
            vmem_limit_bytes=VMEM_LIMIT),
        name="fft",
    )(u.reshape(ngroups, b, n1, n2, c), g_t, h_t, cs_t)
    return f.reshape(ngroups, b, s, c)


def _ssm_tables(lam_re, lam_im, log_dt, b_re, b_im, c_re, c_im, d_skip, seg_len):
    assert PAIR == 2
    ndir, ng, p = lam_re.shape
    hd = b_re.shape[-1]
    r = SCAN_BLOCK
    nq = ng // PAIR
    lam_re = lam_re.astype(F32)
    lam_im = lam_im.astype(F32)
    dt = jnp.exp(log_dt.astype(F32))[..., None]

    def apow(k):
        mag = jnp.exp(lam_re * dt * k)
        return mag * jnp.cos(lam_im * dt * k), mag * jnp.sin(lam_im * dt * k)

    ar, ai = apow(1.0)
    den = lam_re * lam_re + lam_im * lam_im
    qr = ((ar - 1.0) * lam_re + ai * lam_im) / den
    qi = (ai * lam_re - (ar - 1.0) * lam_im) / den
    bt_re = jnp.swapaxes(b_re.astype(F32), -1, -2)
    bt_im = jnp.swapaxes(b_im.astype(F32), -1, -2)
    bre = qr[:, :, None, :] * bt_re - qi[:, :, None, :] * bt_im
    bim = qr[:, :, None, :] * bt_im + qi[:, :, None, :] * bt_re
    cr = c_re.astype(F32)
    ci = c_im.astype(F32)

    ks = jnp.arange(r + 1, dtype=F32).reshape(r + 1, 1, 1, 1)
    pr, pi = apow(ks)
    abr = pr[:r, :, :, None, :] * bre - pi[:r, :, :, None, :] * bim
    abi = pr[:r, :, :, None, :] * bim + pi[:r, :, :, None, :] * bre

    def in_rows(x):
        return jnp.transpose(x, (1, 0, 2, 3)).reshape(ng, r * hd, p)

    bp_re = jnp.stack([in_rows(abr[::-1, 0]), in_rows(abr[:, 1])])
    bp_im = jnp.stack([in_rows(abi[::-1, 0]), in_rows(abi[:, 1])])

    klag = jnp.sum(abr[:, :, :, :, None, :] * cr[None, :, :, None, :, :]
                   - abi[:, :, :, :, None, :] * ci[None, :, :, None, :, :], axis=-1)
    kzero = jnp.zeros_like(klag[0, 0])
    kin = jnp.stack(
        [jnp.concatenate(
            [(klag[i - j, 0] if i >= j else kzero) + (klag[j - i, 1] if j >= i else kzero)
             for i in range(r)], axis=-1) for j in range(r)], axis=1)
    kin = kin.reshape(ng, r * hd, r * hd)
    dsk = d_skip.astype(F32).reshape(ng, hd)
    kin = kin + jnp.eye(r * hd, dtype=F32)[None] * jnp.tile(dsk, (1, r))[:, None, :]

    ct_re = jnp.swapaxes(cr, -1, -2)
    ct_im = jnp.swapaxes(ci, -1, -2)

    def state_out(d, pows):
        er = jnp.concatenate([ct_re[d] * pr[k, d][:, :, None] - ct_im[d] * pi[k, d][:, :, None]
                              for k in pows], axis=-1)
        ei = jnp.concatenate([ct_re[d] * pi[k, d][:, :, None] + ct_im[d] * pr[k, d][:, :, None]
                              for k in pows], axis=-1)
        return er, -ei

    cf_re, cf_im = state_out(0, [i + 1 for i in range(r)])
    cb_re, cb_im = state_out(1, [r - i for i in range(r)])

    def pair_bd(m):
        lead = m.shape[:-3]
        rr, cc = m.shape[-2:]
        m = m.reshape(lead + (nq, PAIR, rr, cc))
        m0, m1 = m[..., 0, :, :], m[..., 1, :, :]
        z = jnp.zeros_like(m0)
        return jnp.concatenate([jnp.concatenate([m0, z], axis=-1),
                                jnp.concatenate([z, m1], axis=-1)], axis=-2)

    bq = jnp.concatenate([pair_bd(bp_re), pair_bd(bp_im)], axis=-1).astype(BF16)
    wy = jnp.concatenate([pair_bd(kin), pair_bd(cf_re), pair_bd(cf_im),
                          pair_bd(cb_re), pair_bd(cb_im)], axis=-2).astype(BF16)

    def lanes(v):
        v = v.reshape(ndir, nq, 1, PAIR * p)
        return jnp.broadcast_to(v, (ndir, nq, SUBLANES, PAIR * p))

    a8r, a8i = apow(float(r))
    alr, ali = apow(float(seg_len))
    return bq, wy, lanes(a8r), lanes(a8i), lanes(alr), lanes(ali)


def _scan_kernel(*refs, mc, nq, reverse, chain, emit):
    u8_ref, e_ref, bq_ref, a8re_ref, a8im_ref, alre_ref, alim_ref = refs[:7]
    if emit == "y":
        xo_ref, wy_ref, y8_ref, eo_ref, d_scr, st_scr = refs[7:]
    elif emit == "x":
        x_ref, eo_ref, d_scr, st_scr = refs[7:]
    else:
        eo_ref, d_scr, st_scr = refs[7:]
    i = pl.program_id(0)
    n = pl.num_programs(0)
    rows = mc * SUBLANES
    hl = LANES

    @pl.when(i == 0)
    def _init():
        if not chain:
            st_scr[...] = jnp.zeros(st_scr.shape, F32)
        else:
            row = lax.broadcasted_iota(jnp.int32, (SUBLANES, hl), 0)
            shift = SUBLANES - 1 if reverse else 1
            keep = (row <= SUBLANES - 2) if reverse else (row >= 1)
            for q in range(nq):
                er = pltpu.roll(e_ref[q, :, 0:hl], shift, 0)
                ei = pltpu.roll(e_ref[q, :, hl:2 * hl], shift, 0)
                lr = alre_ref[q]
                li = alim_ref[q]
                xr = jnp.zeros((SUBLANES, hl), F32)
                xi = jnp.zeros((SUBLANES, hl), F32)
                for _ in range(SUBLANES - 1):
                    pr = pltpu.roll(xr, shift, 0)
                    pi = pltpu.roll(xi, shift, 0)
                    xr = jnp.where(keep, lr * pr - li * pi + er, 0.0)
                    xi = jnp.where(keep, lr * pi + li * pr + ei, 0.0)
                st_scr[q, :, 0:hl] = xr
                st_scr[q, :, hl:2 * hl] = xi

    for q0 in range(0, nq, SCAN_INTERLEAVE):
        qs = list(range(q0, min(q0 + SCAN_INTERLEAVE, nq)))
        for k, q in enumerate(qs):
            d_scr[k] = _dot(u8_ref[q], bq_ref[q])
        a_re = [a8re_ref[q] for q in qs]
        a_im = [a8im_ref[q] for q in qs]

        def body(step, carry, a_re=a_re, a_im=a_im, nk=len(qs)):
            m = (mc - 1 - step) if reverse else step
            r = pl.multiple_of(m * SUBLANES, SUBLANES)
            out = []
            for k in range(nk):
                re, im = carry[2 * k], carry[2 * k + 1]
                d_re = d_scr[k, pl.ds(r, SUBLANES), 0:hl]
                d_im = d_scr[k, pl.ds(r, SUBLANES), hl:2 * hl]
                d_scr[k, pl.ds(r, SUBLANES), 0:hl] = re
                d_scr[k, pl.ds(r, SUBLANES), hl:2 * hl] = im
                out.append(a_re[k] * re - a_im[k] * im + d_re)
                out.append(a_re[k] * im + a_im[k] * re + d_im)
            return tuple(out)

        init = []
        for q in qs:
            init += [st_scr[q, :, 0:hl], st_scr[q, :, hl:2 * hl]]
        fin = lax.fori_loop(0, mc, body, tuple(init), unroll=2)
        for k, q in enumerate(qs):
            st_scr[q, :, 0:hl] = fin[2 * k]
            st_scr[q, :, hl:2 * hl] = fin[2 * k + 1]
            if emit == "x":
                x_ref[q] = d_scr[k].astype(BF16)
            elif emit == "y":
                lhs = jnp.concatenate([u8_ref[q], xo_ref[q], d_scr[k].astype(BF16)], axis=1)
                y8_ref[q] = _dot(lhs, wy_ref[q]).astype(BF16)

    @pl.when(i == n - 1)
    def _fin():
        eo_ref[...] = st_scr[...]


def _scan(u8, e_in, bq, a8re, a8im, alre, alim, mc, reverse, chain, emit, x_other=None, wy=None):
    nq = bq.shape[0]
    mtot = u8.shape[1] // NB
    n = mtot // mc
    rows = mc * NB
    order = (lambda i: n - 1 - i) if reverse else (lambda i: i)
    st_shape = (nq, SUBLANES, 2 * LANES)
    row_spec = pl.BlockSpec((nq, rows, 2 * LANES), lambda i: (0, order(i), 0))
    row_shape = jax.ShapeDtypeStruct((nq, mtot * NB, 2 * LANES), BF16)
    in_specs = [row_spec, _const_spec(st_shape), _const_spec(bq.shape),
                _const_spec(a8re.shape), _const_spec(a8im.shape),
                _const_spec(alre.shape), _const_spec(alim.shape)]
    args = [u8, e_in, bq, a8re, a8im, alre, alim]
    out_specs = [_const_spec(st_shape)]
    out_shape = [jax.ShapeDtypeStruct(st_shape, F32)]
    if emit == "y":
        in_specs += [row_spec, _const_spec(wy.shape)]
        args += [x_other, wy]
    if emit is not None:
        out_specs = [row_spec] + out_specs
        out_shape = [row_shape] + out_shape
    kern = functools.partial(_scan_kernel, mc=mc, nq=nq, reverse=reverse, chain=chain, emit=emit)
    return pl.pallas_call(
        kern,
        grid=(n,),
        in_specs=in_specs,
        out_specs=out_specs,
        out_shape=out_shape,
        scratch_shapes=[pltpu.VMEM((SCAN_INTERLEAVE, rows, 2 * LANES), F32),
                        pltpu.VMEM(st_shape, F32)],
        compiler_params=pltpu.CompilerParams(
            dimension_semantics=("arbitrary",), vmem_limit_bytes=VMEM_LIMIT),
        name="ssm_scan_" + ("bwd" if reverse else "fwd") + ("_chain" if chain else "")
             + ("_" + emit if emit else "_end"),
    )(*args)


def _ssm(u8, tables, chained, mc):
    bq, wy, a8re, a8im, alre, alim = tables
    nq = bq.shape[1]
    ef = eb = jnp.zeros((nq, SUBLANES, 2 * LANES), F32)
    par = lambda dr: (bq[dr], a8re[dr], a8im[dr], alre[dr], alim[dr])
    if chained:
        (ef,) = _scan(u8, ef, *par(0), mc, False, False, None)
        (eb,) = _scan(u8, eb, *par(1), mc, True, False, None)
    xf, _ = _scan(u8, ef, *par(0), mc, False, chained, "x")
    y8, _ = _scan(u8, eb, *par(1), mc, True, chained, "y", xf, wy)
    return y8


def _ffn_chunks(hidden, step=1024):
    edges = list(range(0, hidden, step)) + [hidden]
    return list(zip(edges[:-1], edges[1:]))


def _mix_kernel(x_ref, f_ref, y8_ref, sa_ref, sb_ref, gpost_ref,
                wfo_ref, wval_ref, wgate_ref, wout_ref, o_ref, y_scr, *, tt, d, fw, sw):
    rows = NB * tt
    qpb = GROUPS_PER_BLOCK // PAIR
    for cb in range(sw // LANES):
        vs = [y8_ref[cb * qpb + gl // PAIR, :, (gl % PAIR) * LANES:(gl % PAIR + 1) * LANES].astype(F32)
              for gl in range(GROUPS_PER_BLOCK)]
        unfolded = _transpose_pieces(vs)
        for i in range(SCAN_BLOCK):
            for m in range(tt // SCAN_BLOCK):
                y_scr[cb, pl.ds((SCAN_BLOCK * m + i) * NB, NB), :] = unfolded[i][m * NB:(m + 1) * NB, :]
    ys = jnp.concatenate(
        [jnp.concatenate(
            [y_scr[cb, pl.ds(b, tt, stride=NB), :] for b in range(NB)], axis=0)
         for cb in range(sw // LANES)], axis=1)
    z = jax.nn.gelu(ys).astype(BF16)
    f = jnp.concatenate(
        [f_ref[g].reshape(rows, LANES) for g in range(fw // LANES)], axis=1).astype(BF16)
    br_a = _dot(f, wfo_ref[...])
    br_b = _dot(z, wval_ref[...]) * _sigmoid(_dot(z, wgate_ref[...]))
    sa = sa_ref[...].reshape(rows, d).astype(F32)
    sb = sb_ref[...].reshape(rows, d).astype(F32)
    merged = (sa * br_a + sb * br_b).astype(BF16)
    m = _dot(merged, wout_ref[...])
    o_ref[...] = (x_ref[...].reshape(rows, d) + _rms(m, gpost_ref[...])).reshape(NB, tt, d)


def _mix(xv, f, y8, sa, sb, gpost, wfo, wval, wgate, wout, tt):
    nb, sseg, d = xv.shape
    fw = f.shape[0] * LANES
    nq = y8.shape[0]
    sw = nq * PAIR * SSM_GROUP_DIM
    n = sseg // tt
    assert tt % SCAN_BLOCK == 0
    kern = functools.partial(_mix_kernel, tt=tt, d=d, fw=fw, sw=sw)
    tok = lambda width: pl.BlockSpec((NB, tt, width), lambda i: (0, i, 0))
    tmaj = pl.BlockSpec((nq, tt, PAIR * LANES), lambda i: (0, i, 0))
    return pl.pallas_call(
        kern,
        grid=(n,),
        in_specs=[tok(d), pl.BlockSpec((fw // LANES, NB, tt, LANES), lambda i: (0, 0, i, 0)),
                  tmaj, tok(d), tok(d), _const_spec((1, d)),
                  _const_spec(wfo.shape), _const_spec(wval.shape), _const_spec(wgate.shape),
                  _const_spec(wout.shape)],
        out_specs=tok(d),
        out_shape=jax.ShapeDtypeStruct((NB, sseg, d), F32),
        scratch_shapes=[pltpu.VMEM((sw // LANES, NB * tt, LANES), F32)],
        compiler_params=pltpu.CompilerParams(
            dimension_semantics=("arbitrary",), vmem_limit_bytes=VMEM_LIMIT),
        name="mix",
    )(xv, f, y8, sa, sb, gpost, wfo, wval, wgate, wout)


def _ffn_kernel(x_ref, gpre_ref, gpost_ref, wg_ref, wu_ref, wd_ref, o_ref, *, tt, d, hidden):
    rows = NB * tt
    x1 = x_ref[...].reshape(rows, d)
    h2 = _rms(x1, gpre_ref[...]).astype(BF16)
    acc = None
    for c0, c1 in _ffn_chunks(hidden):
        g = _dot(h2, wg_ref[:, c0:c1])
        u = _dot(h2, wu_ref[:, c0:c1])
        part = _dot((g * _sigmoid(g) * u).astype(BF16), wd_ref[c0:c1, :])
        acc = part if acc is None else acc + part
    o_ref[...] = (x1 + _rms(acc, gpost_ref[...])).reshape(NB, tt, d)


def _ffn(xv, gpre, gpost, wg, wu, wd, tt):
    nb, sseg, d = xv.shape
    hidden = wg.shape[-1]
    tok = pl.BlockSpec((NB, tt, d), lambda i: (0, i, 0))
    return pl.pallas_call(
        functools.partial(_ffn_kernel, tt=tt, d=d, hidden=hidden),
        grid=(sseg // tt,),
        in_specs=[tok, _const_spec((1, d)), _const_spec((1, d)),
                  _const_spec(wg.shape), _const_spec(wu.shape), _const_spec(wd.shape)],
        out_specs=tok,
        out_shape=jax.ShapeDtypeStruct((NB, sseg, d), F32),
        compiler_params=pltpu.CompilerParams(
            dimension_semantics=("arbitrary",), vmem_limit_bytes=VMEM_LIMIT),
        name="ffn",
    )(xv, gpre, gpost, wg, wu, wd)


def _pick(total, want):
    t = min(want, total)
    assert total % t == 0, (total, t)
    return t


def _encoder_layer(x, p):
    b, s, d = x.shape
    assert b in (1, NB), "one sequence (split in NB segments) or NB sequences"
    sseg = (b * s) // NB
    fw = p["w_fnet_out"].shape[0]
    sw = p["w_glu_val"].shape[0]
    xv = x.reshape(NB, sseg, d)
    tt = _pick(sseg, TOKEN_TILE_T)

    uf, u8, sa, sb = _in_proj(xv, p["norm_mix_pre"], p["w_in"], fw, sw, tt)
    ng = fw // LANES
    f = _fourier_mix(uf.reshape(ng, b, s, LANES)).reshape(ng, NB, sseg, LANES)

    tables = _ssm_tables(p["lam_re"], p["lam_im"], p["log_dt"], p["b_re"], p["b_im"],
                         p["c_re"], p["c_im"], p["d_skip"], sseg)
    ys = _ssm(u8, tables, chained=(b == 1), mc=_pick(sseg // SCAN_BLOCK, SCAN_CHUNK_BLOCKS))

    x1 = _mix(xv, f, ys, sa, sb, p["norm_mix_post"],
              p["w_fnet_out"], p["w_glu_val"], p["w_glu_gate"], p["w_out"], tt)
    y = _ffn(x1, p["norm_ffn_pre"], p["norm_ffn_post"],
             p["w_ffn_gate"], p["w_ffn_up"], p["w_ffn_down"], tt)
    return y.reshape(b, s, d)


def _layer_params(l, norm_mix_pre, norm_mix_post, norm_ffn_pre, norm_ffn_post, w_in, w_fnet_out,
                  lam_re, lam_im, log_dt, b_re, b_im, c_re, c_im, d_skip, w_glu_val, w_glu_gate,
                  w_out, w_ffn_gate, w_ffn_up, w_ffn_down):
    row = lambda v: v[l].astype(F32).reshape(1, -1)
    w = lambda v: v[l].astype(BF16)
    return dict(
        norm_mix_pre=row(norm_mix_pre), norm_mix_post=row(norm_mix_post),
        norm_ffn_pre=row(norm_ffn_pre), norm_ffn_post=row(norm_ffn_post),
        w_in=w(w_in), w_fnet_out=w(w_fnet_out),
        lam_re=lam_re[l], lam_im=lam_im[l], log_dt=log_dt[l],
        b_re=b_re[l], b_im=b_im[l], c_re=c_re[l], c_im=c_im[l], d_skip=row(d_skip),
        w_glu_val=w(w_glu_val), w_glu_gate=w(w_glu_gate), w_out=w(w_out),
        w_ffn_gate=w(w_ffn_gate), w_ffn_up=w(w_ffn_up), w_ffn_down=w(w_ffn_down))


def kernel(x_prompt, x_sample, norm_mix_pre, norm_mix_post, norm_ffn_pre, norm_ffn_post, w_in, w_fnet_out, lam_re, lam_im, log_dt, b_re, b_im, c_re, c_im, d_skip, w_glu_val, w_glu_gate, w_out, w_ffn_gate, w_ffn_up, w_ffn_down):
    weights = (norm_mix_pre, norm_mix_post, norm_ffn_pre, norm_ffn_post, w_in, w_fnet_out,
               lam_re, lam_im, log_dt, b_re, b_im, c_re, c_im, d_skip, w_glu_val, w_glu_gate,
               w_out, w_ffn_gate, w_ffn_up, w_ffn_down)
    depth = w_in.shape[0]
    outs = []
    for x in (x_prompt, x_sample):
        for l in range(depth):
            x = _encoder_layer(x, _layer_params(l, *weights))
        outs.append(x)
    return tuple(outs)
```

```python
import functools
import math

import numpy as np
import jax
import jax.numpy as jnp
from jax import lax
from jax.experimental import pallas as pl
from jax.experimental.pallas import tpu as pltpu

F32 = jnp.float32
BF16 = jnp.bfloat16

EPS = 1e-6
SUBLANES = 8
LANES = 128
NB = SUBLANES

FNET_GROUP_DIM = 128
SSM_GROUP_DIM = 16
GROUPS_PER_BLOCK = LANES // SSM_GROUP_DIM
SCAN_BLOCK = LANES // SSM_GROUP_DIM
PAIR = 2
SCAN_INTERLEAVE = 8
SCAN_CHUNK_BLOCKS = 64

TOKEN_TILE_T = 128

FFT_N2 = 128
FFT_S2_BLOCK = 16
FFT_K1_BLOCK = 8
FFT_VMEM_BUDGET = 40 * 1024 * 1024
FFT_GROUPS_PER_PASS = 2
FFT_ROW_PAD = SUBLANES
VMEM_LIMIT = 56 * 1024 * 1024


def _dot(a, b):
    return jnp.dot(a, b, preferred_element_type=F32)


def _sigmoid(x):
    return 0.5 * jnp.tanh(0.5 * x) + 0.5


def _rms(x, g):
    return x * lax.rsqrt(jnp.mean(x * x, axis=-1, keepdims=True) + EPS) * g


def _const_spec(shape):
    nd = len(shape)
    return pl.BlockSpec(shape, lambda *_: (0,) * nd, pipeline_mode=pl.Buffered(1))


def _transpose_pieces(vs):
    n = GROUPS_PER_BLOCK
    rows = vs[0].shape[0]
    piece = lax.broadcasted_iota(jnp.int32, (rows, LANES), 1) // SSM_GROUP_DIM
    cur = list(vs)
    d = 1
    while d < n:
        upper = (piece & d) != 0
        nxt = [None] * n
        for c in range(n):
            if c & d == 0:
                lo, hi = cur[c], cur[c | d]
                nxt[c] = jnp.where(upper, pltpu.roll(hi, d * SSM_GROUP_DIM, 1), lo)
                nxt[c | d] = jnp.where(upper, hi, pltpu.roll(lo, LANES - d * SSM_GROUP_DIM, 1))
        cur = nxt
        d *= 2
    return cur


def _in_proj_kernel(x_ref, g_ref, w_ref, uf_ref, u8_ref, sa_ref, sb_ref, us_scr, *, tt, fw, sw, d):
    rows = NB * tt
    o = fw + sw
    x = x_ref[...].reshape(rows, d)
    h = _rms(x, g_ref[...]).astype(BF16)
    uf = _dot(h, w_ref[:, 0:fw])
    for g in range(fw // LANES):
        uf_ref[g] = uf[:, g * LANES:(g + 1) * LANES].reshape(NB, tt, LANES)
    us = _dot(h, w_ref[:, fw:fw + sw])
    sa_ref[...] = _sigmoid(_dot(h, w_ref[:, o:o + d])).reshape(NB, tt, d).astype(BF16)
    sb_ref[...] = _sigmoid(_dot(h, w_ref[:, o + d:o + 2 * d])).reshape(NB, tt, d).astype(BF16)
    for cb in range(sw // LANES):
        for b in range(NB):
            us_scr[cb, pl.ds(b, tt, stride=NB), :] = us[b * tt:(b + 1) * tt, cb * LANES:(cb + 1) * LANES]
    for cb in range(sw // LANES):
        vs = [jnp.concatenate([us_scr[cb, pl.ds((SCAN_BLOCK * m + j) * NB, NB), :]
                               for m in range(tt // SCAN_BLOCK)], axis=0)
              for j in range(SCAN_BLOCK)]
        folded = _transpose_pieces(vs)
        for gl in range(GROUPS_PER_BLOCK):
            g = cb * GROUPS_PER_BLOCK + gl
            u8_ref[g // PAIR, :, (g % PAIR) * LANES:(g % PAIR + 1) * LANES] = folded[gl].astype(BF16)


def _in_proj(xv, gain, w_in, fw, sw, tt):
    nb, sseg, d = xv.shape
    n = sseg // tt
    nq = sw // (PAIR * SSM_GROUP_DIM)
    assert tt % SCAN_BLOCK == 0
    kern = functools.partial(_in_proj_kernel, tt=tt, fw=fw, sw=sw, d=d)
    return pl.pallas_call(
        kern,
        grid=(n,),
        in_specs=[
            pl.BlockSpec((NB, tt, d), lambda i: (0, i, 0)),
            _const_spec((1, d)),
            _const_spec(w_in.shape),
        ],
        out_specs=[
            pl.BlockSpec((fw // LANES, NB, tt, LANES), lambda i: (0, 0, i, 0)),
            pl.BlockSpec((nq, tt, PAIR * LANES), lambda i: (0, i, 0)),
            pl.BlockSpec((NB, tt, d), lambda i: (0, i, 0)),
            pl.BlockSpec((NB, tt, d), lambda i: (0, i, 0)),
        ],
        out_shape=[
            jax.ShapeDtypeStruct((fw // LANES, NB, sseg, LANES), F32),
            jax.ShapeDtypeStruct((nq, sseg, PAIR * LANES), BF16),
            jax.ShapeDtypeStruct((NB, sseg, d), BF16),
            jax.ShapeDtypeStruct((NB, sseg, d), BF16),
        ],
        scratch_shapes=[pltpu.VMEM((sw // LANES, NB * tt, LANES), F32)],
        compiler_params=pltpu.CompilerParams(
            dimension_semantics=("arbitrary",), vmem_limit_bytes=VMEM_LIMIT),
        name="in_proj",
    )(xv, gain, w_in)


def _fft_tables(s):
    n2 = FFT_N2
    n1 = s // n2
    s1 = np.arange(n1, dtype=np.int64)
    k1 = np.arange(n1, dtype=np.int64)
    s2 = np.arange(n2, dtype=np.int64)
    idx = (k1[None, :, None] * s1[None, None, :] * n2 + s2[:, None, None] * k1[None, :, None]) % s
    ang = 2.0 * np.pi * idx.astype(np.float64) / s
    g = np.stack([np.cos(ang), -np.sin(ang)], axis=1) / math.sqrt(n1)
    g = g.reshape(n2, 2 * n1, n1)
    k2 = np.arange(n2, dtype=np.int64)
    ang2 = 2.0 * np.pi * ((k2[:, None] * s2[None, :]) % n2).astype(np.float64) / n2
    c2, s2m = np.cos(ang2), np.sin(ang2)
    h = np.block([[c2, s2m], [-s2m, c2]]) / math.sqrt(n2)
    c = np.arange(FNET_GROUP_DIM, dtype=np.int64)
    angc = 2.0 * np.pi * ((c[:, None] * c[None, :]) % FNET_GROUP_DIM).astype(np.float64) / FNET_GROUP_DIM
    cs = np.concatenate([np.cos(angc), np.sin(angc)], axis=0) / math.sqrt(FNET_GROUP_DIM)
    return g.astype(np.float32), h.astype(np.float32), cs.astype(np.float32)


def _fft_kernel(x_ref, g_ref, h_ref, cs_ref, o_ref, xs_scr, a_scr, os_scr,
                *, n1, n2, ng, p1, s2b, k1b):
    p = pl.program_id(2)
    kstride = 2 * n2 + FFT_ROW_PAD

    @pl.when(p < p1)
    def _stage1():
        xs_scr[...] = x_ref[...].reshape(ng, n1 * s2b, LANES)
        for j in range(s2b):
            xj = jnp.concatenate(
                [xs_scr[g, pl.ds(j, n1, stride=s2b), :] for g in range(ng)], axis=1).astype(BF16)
            r = _dot(g_ref[j], xj)
            s2 = p * s2b + j
            for g in range(ng):
                a_scr[g, pl.ds(s2, n1, stride=kstride), :] = r[0:n1, g * LANES:(g + 1) * LANES]
                a_scr[g, pl.ds(n2 + s2, n1, stride=kstride), :] = r[n1:2 * n1, g * LANES:(g + 1) * LANES]

    @pl.when(p >= p1)
    def _stage2():
        for j in range(k1b):
            row0 = pl.multiple_of(((p - p1) * k1b + j) * kstride, SUBLANES)
            a = jnp.concatenate(
                [a_scr[g, pl.ds(row0, 2 * n2), :] for g in range(ng)], axis=1).astype(BF16)
            y = _dot(h_ref[...], a).astype(BF16)
            for g in range(ng):
                yg = jnp.concatenate(
                    [y[0:n2, g * LANES:(g + 1) * LANES], y[n2:2 * n2, g * LANES:(g + 1) * LANES]], axis=1)
                os_scr[g, pl.ds(j, n2, stride=k1b), :] = _dot(yg, cs_ref[...])
        o_ref[...] = os_scr[...].reshape(ng, n2, k1b, LANES)


def _fourier_mix(u):
    ngroups, b, s, c = u.shape
    assert c == FNET_GROUP_DIM == LANES
    n2 = FFT_N2
    n1 = s // n2
    ng = FFT_GROUPS_PER_PASS
    kstride = 2 * n2 + FFT_ROW_PAD

    def vmem_bytes(s2b, k1b):
        f32 = 4 * ng * c
        return (f32 * n1 * kstride + 3 * f32 * n1 * s2b + 3 * f32 * n2 * k1b
                + 2 * 2 * s2b * 2 * n1 * max(n1, LANES))

    s2b, k1b = FFT_S2_BLOCK, FFT_K1_BLOCK
    while (vmem_bytes(2 * s2b, 2 * k1b) <= FFT_VMEM_BUDGET
           and n2 % (2 * s2b) == 0 and n1 % (2 * k1b) == 0):
        s2b, k1b = 2 * s2b, 2 * k1b
    assert n1 * n2 == s and n1 % k1b == 0 and n2 % s2b == 0 and ngroups % ng == 0
    p1, p2 = n2 // s2b, n1 // k1b
    g_np, h_np, cs_np = _fft_tables(s)
    g_t = jnp.asarray(g_np).astype(BF16)
    h_t = jnp.asarray(h_np).astype(BF16)
    cs_t = jnp.asarray(cs_np).astype(BF16)

    f = pl.pallas_call(
        functools.partial(_fft_kernel, n1=n1, n2=n2, ng=ng, p1=p1, s2b=s2b, k1b=k1b),
        grid=(b, ngroups // ng, p1 + p2),
        in_specs=[
            pl.BlockSpec((ng, None, n1, s2b, c),
                         lambda bi, gi, p: (gi, bi, 0, jnp.minimum(p, p1 - 1), 0)),
            pl.BlockSpec((s2b, 2 * n1, n1), lambda bi, gi, p: (jnp.minimum(p, p1 - 1), 0, 0)),
            _const_spec((2 * n2, 2 * n2)),
            _const_spec((2 * FNET_GROUP_DIM, FNET_GROUP_DIM)),
        ],
        out_specs=pl.BlockSpec((ng, None, n2, k1b, c),
                               lambda bi, gi, p: (gi, bi, 0, jnp.maximum(p - p1, 0), 0)),
        out_shape=jax.ShapeDtypeStruct((ngroups, b, n2, n1, c), F32),
        scratch_shapes=[pltpu.VMEM((ng, n1 * s2b, c), F32),
                        pltpu.VMEM((ng, n1 * kstride, c), F32),
                        pltpu.VMEM((ng, n2 * k1b, c), F32)],
        compiler_params=pltpu.CompilerParams(
            dimension_semantics=("arbitrary", "arbitrary", "arbitrary"),
            vmem_limit_bytes=VMEM_LIMIT),
        name="fft",
    )(u.reshape(ngroups, b, n1, n2, c), g_t, h_t, cs_t)
    return f.reshape(ngroups, b, s, c)


def _ssm_tables(lam_re, lam_im, log_dt, b_re, b_im, c_re, c_im, d_skip, seg_len):
    assert PAIR == 2
    ndir, ng, p = lam_re.shape
    hd = b_re.shape[-1]
    r = SCAN_BLOCK
    nq = ng // PAIR
    lam_re = lam_re.astype(F32)
    lam_im = lam_im.astype(F32)
    dt = jnp.exp(log_dt.astype(F32))[..., None]

    def apow(k):
        mag = jnp.exp(lam_re * dt * k)
        return mag * jnp.cos(lam_im * dt * k), mag * jnp.sin(lam_im * dt * k)

    ar, ai = apow(1.0)
    den = lam_re * lam_re + lam_im * lam_im
    qr = ((ar - 1.0) * lam_re + ai * lam_im) / den
    qi = (ai * lam_re - (ar - 1.0) * lam_im) / den
    bre = qr[..., None] * b_re.astype(F32) - qi[..., None] * b_im.astype(F32)
    bim = qr[..., None] * b_im.astype(F32) + qi[..., None] * b_re.astype(F32)
    cr = c_re.astype(F32)
    ci = c_im.astype(F32)

    ks = jnp.arange(r + 1, dtype=F32).reshape(r + 1, 1, 1, 1)
    pr, pi = apow(ks)
    abr = pr[:r, ..., None] * bre - pi[:r, ..., None] * bim
    abi = pr[:r, ..., None] * bim + pi[:r, ..., None] * bre

    def in_rows(x):
        return jnp.transpose(x, (1, 0, 3, 2)).reshape(ng, r * hd, p)

    bp_re = jnp.stack([in_rows(abr[::-1, 0]), in_rows(abr[:, 1])])
    bp_im = jnp.stack([in_rows(abi[::-1, 0]), in_rows(abi[:, 1])])

    klag = jnp.sum(cr[None, :, :, :, :, None] * abr[:, :, :, None, :, :]
                   - ci[None, :, :, :, :, None] * abi[:, :, :, None, :, :], axis=4)
    kzero = jnp.zeros_like(klag[0, 0])
    kin = jnp.stack([
        jnp.stack([(klag[i - j, 0] if i >= j else kzero) + (klag[j - i, 1] if j >= i else kzero)
                   for i in range(r)]) for j in range(r)])
    kin = jnp.transpose(kin, (2, 0, 4, 1, 3)).reshape(ng, r * hd, r * hd)
    dsk = d_skip.astype(F32).reshape(ng, hd)
    kin = kin + jnp.eye(r * hd, dtype=F32)[None] * jnp.tile(dsk, (1, r))[:, None, :]

    def state_out(d, pows):
        er = cr[d][None] * pr[pows, d][:, :, None, :] - ci[d][None] * pi[pows, d][:, :, None, :]
        ei = cr[d][None] * pi[pows, d][:, :, None, :] + ci[d][None] * pr[pows, d][:, :, None, :]
        to_rows = lambda e: jnp.transpose(e, (1, 3, 0, 2)).reshape(ng, p, r * hd)
        return to_rows(er), -to_rows(ei)

    cf_re, cf_im = state_out(0, jnp.arange(1, r + 1))
    cb_re, cb_im = state_out(1, r - jnp.arange(r))

    def pair_bd(m):
        lead = m.shape[:-3]
        rr, cc = m.shape[-2:]
        m = m.reshape(lead + (nq, PAIR, rr, cc))
        m0, m1 = m[..., 0, :, :], m[..., 1, :, :]
        z = jnp.zeros_like(m0)
        return jnp.concatenate([jnp.concatenate([m0, z], axis=-1),
                                jnp.concatenate([z, m1], axis=-1)], axis=-2)

    bq = jnp.concatenate([pair_bd(bp_re), pair_bd(bp_im)], axis=-1).astype(BF16)
    wy = jnp.concatenate([pair_bd(kin), pair_bd(cf_re), pair_bd(cf_im),
                          pair_bd(cb_re), pair_bd(cb_im)], axis=-2).astype(BF16)

    def lanes(v):
        v = v.reshape(ndir, nq, 1, PAIR * p)
        return jnp.broadcast_to(v, (ndir, nq, SUBLANES, PAIR * p))

    a8r, a8i = apow(float(r))
    alr, ali = apow(float(seg_len))
    return bq, wy, lanes(a8r), lanes(a8i), lanes(alr), lanes(ali)


def _scan_kernel(*refs, mc, nq, reverse, chain, emit):
    u8_ref, e_ref, bq_ref, a8re_ref, a8im_ref, alre_ref, alim_ref = refs[:7]
    if emit == "y":
        xo_ref, wy_ref, y8_ref, eo_ref, d_scr, st_scr = refs[7:]
    elif emit == "x":
        x_ref, eo_ref, d_scr, st_scr = refs[7:]
    else:
        eo_ref, d_scr, st_scr = refs[7:]
    i = pl.program_id(0)
    n = pl.num_programs(0)
    rows = mc * SUBLANES
    hl = LANES

    @pl.when(i == 0)
    def _init():
        if not chain:
            st_scr[...] = jnp.zeros(st_scr.shape, F32)
        else:
            row = lax.broadcasted_iota(jnp.int32, (SUBLANES, hl), 0)
            shift = SUBLANES - 1 if reverse else 1
            keep = (row <= SUBLANES - 2) if reverse else (row >= 1)
            for q in range(nq):
                er = pltpu.roll(e_ref[q, :, 0:hl], shift, 0)
                ei = pltpu.roll(e_ref[q, :, hl:2 * hl], shift, 0)
                lr = alre_ref[q]
                li = alim_ref[q]
                xr = jnp.zeros((SUBLANES, hl), F32)
                xi = jnp.zeros((SUBLANES, hl), F32)
                for _ in range(SUBLANES - 1):
                    pr = pltpu.roll(xr, shift, 0)
                    pi = pltpu.roll(xi, shift, 0)
                    xr = jnp.where(keep, lr * pr - li * pi + er, 0.0)
                    xi = jnp.where(keep, lr * pi + li * pr + ei, 0.0)
                st_scr[q, :, 0:hl] = xr
                st_scr[q, :, hl:2 * hl] = xi

    for q0 in range(0, nq, SCAN_INTERLEAVE):
        qs = list(range(q0, min(q0 + SCAN_INTERLEAVE, nq)))
        for k, q in enumerate(qs):
            d_scr[k] = _dot(u8_ref[q], bq_ref[q])
        a_re = [a8re_ref[q] for q in qs]
        a_im = [a8im_ref[q] for q in qs]

        def body(step, carry, a_re=a_re, a_im=a_im, nk=len(qs)):
            m = (mc - 1 - step) if reverse else step
            r = pl.multiple_of(m * SUBLANES, SUBLANES)
            out = []
            for k in range(nk):
                re, im = carry[2 * k], carry[2 * k + 1]
                d_re = d_scr[k, pl.ds(r, SUBLANES), 0:hl]
                d_im = d_scr[k, pl.ds(r, SUBLANES), hl:2 * hl]
                d_scr[k, pl.ds(r, SUBLANES), 0:hl] = re
                d_scr[k, pl.ds(r, SUBLANES), hl:2 * hl] = im
                out.append(a_re[k] * re - a_im[k] * im + d_re)
                out.append(a_re[k] * im + a_im[k] * re + d_im)
            return tuple(out)

        init = []
        for q in qs:
            init += [st_scr[q, :, 0:hl], st_scr[q, :, hl:2 * hl]]
        fin = lax.fori_loop(0, mc, body, tuple(init), unroll=2)
        for k, q in enumerate(qs):
            st_scr[q, :, 0:hl] = fin[2 * k]
            st_scr[q, :, hl:2 * hl] = fin[2 * k + 1]
            if emit == "x":
                x_ref[q] = d_scr[k].astype(BF16)
            elif emit == "y":
                lhs = jnp.concatenate([u8_ref[q], xo_ref[q], d_scr[k].astype(BF16)], axis=1)
                y8_ref[q] = _dot(lhs, wy_ref[q]).astype(BF16)

    @pl.when(i == n - 1)
    def _fin():
        eo_ref[...] = st_scr[...]


def _scan(u8, e_in, bq, a8re, a8im, alre, alim, mc, reverse, chain, emit, x_other=None, wy=None):
    nq = bq.shape[0]
    mtot = u8.shape[1] // NB
    n = mtot // mc
    rows = mc * NB
    order = (lambda i: n - 1 - i) if reverse else (lambda i: i)
    st_shape = (nq, SUBLANES, 2 * LANES)
    row_spec = pl.BlockSpec((nq, rows, 2 * LANES), lambda i: (0, order(i), 0))
    row_shape = jax.ShapeDtypeStruct((nq, mtot * NB, 2 * LANES), BF16)
    in_specs = [row_spec, _const_spec(st_shape), _const_spec(bq.shape),
                _const_spec(a8re.shape), _const_spec(a8im.shape),
                _const_spec(alre.shape), _const_spec(alim.shape)]
    args = [u8, e_in, bq, a8re, a8im, alre, alim]
    out_specs = [_const_spec(st_shape)]
    out_shape = [jax.ShapeDtypeStruct(st_shape, F32)]
    if emit == "y":
        in_specs += [row_spec, _const_spec(wy.shape)]
        args += [x_other, wy]
    if emit is not None:
        out_specs = [row_spec] + out_specs
        out_shape = [row_shape] + out_shape
    kern = functools.partial(_scan_kernel, mc=mc, nq=nq, reverse=reverse, chain=chain, emit=emit)
    return pl.pallas_call(
        kern,
        grid=(n,),
        in_specs=in_specs,
        out_specs=out_specs,
        out_shape=out_shape,
        scratch_shapes=[pltpu.VMEM((SCAN_INTERLEAVE, rows, 2 * LANES), F32),
                        pltpu.VMEM(st_shape, F32)],
        compiler_params=pltpu.CompilerParams(
            dimension_semantics=("arbitrary",), vmem_limit_bytes=VMEM_LIMIT),
        name="ssm_scan_" + ("bwd" if reverse else "fwd") + ("_chain" if chain else "")
             + ("_" + emit if emit else "_end"),
    )(*args)


def _ssm(u8, tables, chained, mc):
    bq, wy, a8re, a8im, alre, alim = tables
    nq = bq.shape[1]
    ef = eb = jnp.zeros((nq, SUBLANES, 2 * LANES), F32)
    par = lambda dr: (bq[dr], a8re[dr], a8im[dr], alre[dr], alim[dr])
    if chained:
        (ef,) = _scan(u8, ef, *par(0), mc, False, False, None)
        (eb,) = _scan(u8, eb, *par(1), mc, True, False, None)
    xf, _ = _scan(u8, ef, *par(0), mc, False, chained, "x")
    y8, _ = _scan(u8, eb, *par(1), mc, True, chained, "y", xf, wy)
    return y8


def _ffn_chunks(hidden, step=1024):
    edges = list(range(0, hidden, step)) + [hidden]
    return list(zip(edges[:-1], edges[1:]))


def _mix_kernel(x_ref, f_ref, y8_ref, sa_ref, sb_ref, gpost_ref,
                wfo_ref, wval_ref, wgate_ref, wout_ref, o_ref, y_scr, *, tt, d, fw, sw):
    rows = NB * tt
    qpb = GROUPS_PER_BLOCK // PAIR
    for cb in range(sw // LANES):
        vs = [y8_ref[cb * qpb + gl // PAIR, :, (gl % PAIR) * LANES:(gl % PAIR + 1) * LANES].astype(F32)
              for gl in range(GROUPS_PER_BLOCK)]
        unfolded = _transpose_pieces(vs)
        for i in range(SCAN_BLOCK):
            for m in range(tt // SCAN_BLOCK):
                y_scr[cb, pl.ds((SCAN_BLOCK * m + i) * NB, NB), :] = unfolded[i][m * NB:(m + 1) * NB, :]
    ys = jnp.concatenate(
        [jnp.concatenate(
            [y_scr[cb, pl.ds(b, tt, stride=NB), :] for b in range(NB)], axis=0)
         for cb in range(sw // LANES)], axis=1)
    z = jax.nn.gelu(ys).astype(BF16)
    f = jnp.concatenate(
        [f_ref[g].reshape(rows, LANES) for g in range(fw // LANES)], axis=1).astype(BF16)
    br_a = _dot(f, wfo_ref[...])
    br_b = _dot(z, wval_ref[...]) * _sigmoid(_dot(z, wgate_ref[...]))
    sa = sa_ref[...].reshape(rows, d).astype(F32)
    sb = sb_ref[...].reshape(rows, d).astype(F32)
    merged = (sa * br_a + sb * br_b).astype(BF16)
    m = _dot(merged, wout_ref[...])
    o_ref[...] = (x_ref[...].reshape(rows, d) + _rms(m, gpost_ref[...])).reshape(NB, tt, d)


def _mix(xv, f, y8, sa, sb, gpost, wfo, wval, wgate, wout, tt):
    nb, sseg, d = xv.shape
    fw = f.shape[0] * LANES
    nq = y8.shape[0]
    sw = nq * PAIR * SSM_GROUP_DIM
    n = sseg // tt
    assert tt % SCAN_BLOCK == 0
    kern = functools.partial(_mix_kernel, tt=tt, d=d, fw=fw, sw=sw)
    tok = lambda width: pl.BlockSpec((NB, tt, width), lambda i: (0, i, 0))
    tmaj = pl.BlockSpec((nq, tt, PAIR * LANES), lambda i: (0, i, 0))
    return pl.pallas_call(
        kern,
        grid=(n,),
        in_specs=[tok(d), pl.BlockSpec((fw // LANES, NB, tt, LANES), lambda i: (0, 0, i, 0)),
                  tmaj, tok(d), tok(d), _const_spec((1, d)),
                  _const_spec(wfo.shape), _const_spec(wval.shape), _const_spec(wgate.shape),
                  _const_spec(wout.shape)],
        out_specs=tok(d),
        out_shape=jax.ShapeDtypeStruct((NB, sseg, d), F32),
        scratch_shapes=[pltpu.VMEM((sw // LANES, NB * tt, LANES), F32)],
        compiler_params=pltpu.CompilerParams(
            dimension_semantics=("arbitrary",), vmem_limit_bytes=VMEM_LIMIT),
        name="mix",
    )(xv, f, y8, sa, sb, gpost, wfo, wval, wgate, wout)


def _ffn_kernel(x_ref, gpre_ref, gpost_ref, wg_ref, wu_ref, wd_ref, o_ref, *, tt, d, hidden):
    rows = NB * tt
    x1 = x_ref[...].reshape(rows, d)
    h2 = _rms(x1, gpre_ref[...]).astype(BF16)
    acc = None
    for c0, c1 in _ffn_chunks(hidden):
        g = _dot(h2, wg_ref[:, c0:c1])
        u = _dot(h2, wu_ref[:, c0:c1])
        part = _dot((g * _sigmoid(g) * u).astype(BF16), wd_ref[c0:c1, :])
        acc = part if acc is None else acc + part
    o_ref[...] = (x1 + _rms(acc, gpost_ref[...])).reshape(NB, tt, d)


def _ffn(xv, gpre, gpost, wg, wu, wd, tt):
    nb, sseg, d = xv.shape
    hidden = wg.shape[-1]
    tok = pl.BlockSpec((NB, tt, d), lambda i: (0, i, 0))
    return pl.pallas_call(
        functools.partial(_ffn_kernel, tt=tt, d=d, hidden=hidden),
        grid=(sseg // tt,),
        in_specs=[tok, _const_spec((1, d)), _const_spec((1, d)),
                  _const_spec(wg.shape), _const_spec(wu.shape), _const_spec(wd.shape)],
        out_specs=tok,
        out_shape=jax.ShapeDtypeStruct((NB, sseg, d), F32),
        compiler_params=pltpu.CompilerParams(
            dimension_semantics=("arbitrary",), vmem_limit_bytes=VMEM_LIMIT),
        name="ffn",
    )(xv, gpre, gpost, wg, wu, wd)


def _pick(total, want):
    t = min(want, total)
    assert total % t == 0, (total, t)
    return t


def _encoder_layer(x, p):
    b, s, d = x.shape
    assert b in (1, NB), "one sequence (split in NB segments) or NB sequences"
    sseg = (b * s) // NB
    fw = p["w_fnet_out"].shape[0]
    sw = p["w_glu_val"].shape[0]
    xv = x.reshape(NB, sseg, d)
    tt = _pick(sseg, TOKEN_TILE_T)

    uf, u8, sa, sb = _in_proj(xv, p["norm_mix_pre"], p["w_in"], fw, sw, tt)
    ng = fw // LANES
    f = _fourier_mix(uf.reshape(ng, b, s, LANES)).reshape(ng, NB, sseg, LANES)

    tables = _ssm_tables(p["lam_re"], p["lam_im"], p["log_dt"], p["b_re"], p["b_im"],
                         p["c_re"], p["c_im"], p["d_skip"], sseg)
    ys = _ssm(u8, tables, chained=(b == 1), mc=_pick(sseg // SCAN_BLOCK, SCAN_CHUNK_BLOCKS))

    x1 = _mix(xv, f, ys, sa, sb, p["norm_mix_post"],
              p["w_fnet_out"], p["w_glu_val"], p["w_glu_gate"], p["w_out"], tt)
    y = _ffn(x1, p["norm_ffn_pre"], p["norm_ffn_post"],
             p["w_ffn_gate"], p["w_ffn_up"], p["w_ffn_down"], tt)
    return y.reshape(b, s, d)


def _layer_params(l, norm_mix_pre, norm_mix_post, norm_ffn_pre, norm_ffn_post, w_in, w_fnet_out,
                  lam_re, lam_im, log_dt, b_re, b_im, c_re, c_im, d_skip, w_glu_val, w_glu_gate,
                  w_out, w_ffn_gate, w_ffn_up, w_ffn_down):
    row = lambda v: v[l].astype(F32).reshape(1, -1)
    w = lambda v: v[l].astype(BF16)
    return dict(
        norm_mix_pre=row(norm_mix_pre), norm_mix_post=row(norm_mix_post),
        norm_ffn_pre=row(norm_ffn_pre), norm_ffn_post=row(norm_ffn_post),
        w_in=w(w_in), w_fnet_out=w(w_fnet_out),
        lam_re=lam_re[l], lam_im=lam_im[l], log_dt=log_dt[l],
        b_re=b_re[l], b_im=b_im[l], c_re=c_re[l], c_im=c_im[l], d_skip=row(d_skip),
        w_glu_val=w(w_glu_val), w_glu_gate=w(w_glu_gate), w_out=w(w_out),
        w_ffn_gate=w(w_ffn_gate), w_ffn_up=w(w_ffn_up), w_ffn_down=w(w_ffn_down))


def kernel(x_prompt, x_sample, norm_mix_pre, norm_mix_post, norm_ffn_pre, norm_ffn_post, w_in, w_fnet_out, lam_re, lam_im, log_dt, b_re, b_im, c_re, c_im, d_skip, w_glu_val, w_glu_gate, w_out, w_ffn_gate, w_ffn_up, w_ffn_down):
    weights = (norm_mix_pre, norm_mix_post, norm_ffn_pre, norm_ffn_post, w_in, w_fnet_out,
               lam_re, lam_im, log_dt, b_re, b_im, c_re, c_im, d_skip, w_glu_val, w_glu_gate,
               w_out, w_ffn_gate, w_ffn_up, w_ffn_down)
    depth = w_in.shape[0]
    outs = []
    for x in (x_prompt, x_sample):
        for l in range(depth):
            x = _encoder_layer(x, _layer_params(l, *weights))
        outs.append(x)
    return tuple(outs)
```

```python
import functools
import math

import numpy as np
import jax
import jax.numpy as jnp
from jax import lax
from jax.experimental import pallas as pl
from jax.experimental.pallas import tpu as pltpu

F32 = jnp.float32
BF16 = jnp.bfloat16

EPS = 1e-6
SUBLANES = 8
LANES = 128
NB = SUBLANES

FNET_GROUP_DIM = 128
SSM_GROUP_DIM = 16
GROUPS_PER_BLOCK = LANES // SSM_GROUP_DIM
SCAN_BLOCK = LANES // SSM_GROUP_DIM
PAIR = 2
SCAN_INTERLEAVE = 8
SCAN_CHUNK_BLOCKS = 64

TOKEN_TILE_T = 128

FFT_N2 = 128
FFT_S2_BLOCK = 16
FFT_K1_BLOCK = 8
FFT_VMEM_BUDGET = 40 * 1024 * 1024
FFT_GROUPS_PER_PASS = 2
FFT_ROW_PAD = SUBLANES
VMEM_LIMIT = 56 * 1024 * 1024


def _dot(a, b):
    return jnp.dot(a, b, preferred_element_type=F32)


def _sigmoid(x):
    return 0.5 * jnp.tanh(0.5 * x) + 0.5


def _rms(x, g):
    return x * lax.rsqrt(jnp.mean(x * x, axis=-1, keepdims=True) + EPS) * g


def _const_spec(shape):
    nd = len(shape)
    return pl.BlockSpec(shape, lambda *_: (0,) * nd, pipeline_mode=pl.Buffered(1))


def _transpose_pieces(vs):
    n = GROUPS_PER_BLOCK
    rows = vs[0].shape[0]
    piece = lax.broadcasted_iota(jnp.int32, (rows, LANES), 1) // SSM_GROUP_DIM
    cur = list(vs)
    d = 1
    while d < n:
        upper = (piece & d) != 0
        nxt = [None] * n
        for c in range(n):
            if c & d == 0:
                lo, hi = cur[c], cur[c | d]
                nxt[c] = jnp.where(upper, pltpu.roll(hi, d * SSM_GROUP_DIM, 1), lo)
                nxt[c | d] = jnp.where(upper, hi, pltpu.roll(lo, LANES - d * SSM_GROUP_DIM, 1))
        cur = nxt
        d *= 2
    return cur


def _in_proj_kernel(x_ref, g_ref, w_ref, uf_ref, u8_ref, gates_ref, us_scr, *, tt, fw, sw, d):
    rows = NB * tt
    o = fw + sw
    x = x_ref[...].reshape(rows, d)
    h = _rms(x, g_ref[...]).astype(BF16)
    uf = _dot(h, w_ref[:, 0:fw])
    for g in range(fw // LANES):
        uf_ref[g] = uf[:, g * LANES:(g + 1) * LANES].reshape(NB, tt, LANES)
    us = _dot(h, w_ref[:, fw:fw + sw])
    gates_ref[:, :, 0:d] = _sigmoid(_dot(h, w_ref[:, o:o + d])).reshape(NB, tt, d).astype(BF16)
    gates_ref[:, :, d:2 * d] = _sigmoid(
        _dot(h, w_ref[:, o + d:o + 2 * d])).reshape(NB, tt, d).astype(BF16)
    for cb in range(sw // LANES):
        for b in range(NB):
            us_scr[cb, pl.ds(b, tt, stride=NB), :] = us[b * tt:(b + 1) * tt, cb * LANES:(cb + 1) * LANES]
    for cb in range(sw // LANES):
        vs = [jnp.concatenate([us_scr[cb, pl.ds((SCAN_BLOCK * m + j) * NB, NB), :]
                               for m in range(tt // SCAN_BLOCK)], axis=0)
              for j in range(SCAN_BLOCK)]
        folded = _transpose_pieces(vs)
        for gl in range(GROUPS_PER_BLOCK):
            g = cb * GROUPS_PER_BLOCK + gl
            u8_ref[g // PAIR, :, (g % PAIR) * LANES:(g % PAIR + 1) * LANES] = folded[gl].astype(BF16)


def _in_proj(xv, gain, w_in, fw, sw, tt):
    nb, sseg, d = xv.shape
    n = sseg // tt
    nq = sw // (PAIR * SSM_GROUP_DIM)
    assert tt % SCAN_BLOCK == 0
    kern = functools.partial(_in_proj_kernel, tt=tt, fw=fw, sw=sw, d=d)
    return pl.pallas_call(
        kern,
        grid=(n,),
        in_specs=[
            pl.BlockSpec((NB, tt, d), lambda i: (0, i, 0)),
            _const_spec((1, d)),
            _const_spec(w_in.shape),
        ],
        out_specs=[
            pl.BlockSpec((fw // LANES, NB, tt, LANES), lambda i: (0, 0, i, 0)),
            pl.BlockSpec((nq, tt, PAIR * LANES), lambda i: (0, i, 0)),
            pl.BlockSpec((NB, tt, 2 * d), lambda i: (0, i, 0)),
        ],
        out_shape=[
            jax.ShapeDtypeStruct((fw // LANES, NB, sseg, LANES), F32),
            jax.ShapeDtypeStruct((nq, sseg, PAIR * LANES), BF16),
            jax.ShapeDtypeStruct((NB, sseg, 2 * d), BF16),
        ],
        scratch_shapes=[pltpu.VMEM((sw // LANES, NB * tt, LANES), F32)],
        compiler_params=pltpu.CompilerParams(
            dimension_semantics=("arbitrary",), vmem_limit_bytes=VMEM_LIMIT),
        name="in_proj",
    )(xv, gain, w_in)


def _fft_tables(s):
    n2 = FFT_N2
    n1 = s // n2
    s1 = np.arange(n1, dtype=np.int64)
    k1 = np.arange(n1, dtype=np.int64)
    s2 = np.arange(n2, dtype=np.int64)
    idx = (k1[None, :, None] * s1[None, None, :] * n2 + s2[:, None, None] * k1[None, :, None]) % s
    ang = 2.0 * np.pi * idx.astype(np.float64) / s
    g = np.stack([np.cos(ang), -np.sin(ang)], axis=1) / math.sqrt(n1)
    g = g.reshape(n2, 2 * n1, n1)
    k2 = np.arange(n2, dtype=np.int64)
    ang2 = 2.0 * np.pi * ((k2[:, None] * s2[None, :]) % n2).astype(np.float64) / n2
    c2, s2m = np.cos(ang2), np.sin(ang2)
    h = np.block([[c2, s2m], [-s2m, c2]]) / math.sqrt(n2)
    c = np.arange(FNET_GROUP_DIM, dtype=np.int64)
    angc = 2.0 * np.pi * ((c[:, None] * c[None, :]) % FNET_GROUP_DIM).astype(np.float64) / FNET_GROUP_DIM
    cs = np.concatenate([np.cos(angc), np.sin(angc)], axis=0) / math.sqrt(FNET_GROUP_DIM)
    return g.astype(np.float32), h.astype(np.float32), cs.astype(np.float32)


def _fft_kernel(x_ref, g_ref, h_ref, cs_ref, o_ref, xs_scr, a_scr, os_scr,
                *, n1, n2, ng, p1, s2b, k1b):
    p = pl.program_id(2)
    kstride = 2 * n2 + FFT_ROW_PAD

    @pl.when(p < p1)
    def _stage1():
        xs_scr[...] = x_ref[...].reshape(ng, n1 * s2b, LANES)
        for j in range(s2b):
            xj = jnp.concatenate(
                [xs_scr[g, pl.ds(j, n1, stride=s2b), :] for g in range(ng)], axis=1).astype(BF16)
            r = _dot(g_ref[j], xj)
            s2 = p * s2b + j
            for g in range(ng):
                a_scr[g, pl.ds(s2, n1, stride=kstride), :] = r[0:n1, g * LANES:(g + 1) * LANES]
                a_scr[g, pl.ds(n2 + s2, n1, stride=kstride), :] = r[n1:2 * n1, g * LANES:(g + 1) * LANES]

    @pl.when(p >= p1)
    def _stage2():
        for j in range(k1b):
            row0 = pl.multiple_of(((p - p1) * k1b + j) * kstride, SUBLANES)
            a = jnp.concatenate(
                [a_scr[g, pl.ds(row0, 2 * n2), :] for g in range(ng)], axis=1).astype(BF16)
            y = _dot(h_ref[...], a).astype(BF16)
            for g in range(ng):
                yg = jnp.concatenate(
                    [y[0:n2, g * LANES:(g + 1) * LANES], y[n2:2 * n2, g * LANES:(g + 1) * LANES]], axis=1)
                os_scr[g, pl.ds(j, n2, stride=k1b), :] = _dot(yg, cs_ref[...])
        o_ref[...] = os_scr[...].reshape(ng, n2, k1b, LANES)


def _fourier_mix(u):
    ngroups, b, s, c = u.shape
    assert c == FNET_GROUP_DIM == LANES
    n2 = FFT_N2
    n1 = s // n2
    ng = FFT_GROUPS_PER_PASS
    kstride = 2 * n2 + FFT_ROW_PAD

    def vmem_bytes(s2b, k1b):
        f32 = 4 * ng * c
        return (f32 * n1 * kstride + 3 * f32 * n1 * s2b + 3 * f32 * n2 * k1b
                + 2 * 2 * s2b * 2 * n1 * max(n1, LANES))

    s2b, k1b = FFT_S2_BLOCK, FFT_K1_BLOCK
    while (vmem_bytes(2 * s2b, 2 * k1b) <= FFT_VMEM_BUDGET
           and n2 % (2 * s2b) == 0 and n1 % (2 * k1b) == 0):
        s2b, k1b = 2 * s2b, 2 * k1b
    assert n1 * n2 == s and n1 % k1b == 0 and n2 % s2b == 0 and ngroups % ng == 0
    p1, p2 = n2 // s2b, n1 // k1b
    g_np, h_np, cs_np = _fft_tables(s)
    g_t = jnp.asarray(g_np).astype(BF16)
    h_t = jnp.asarray(h_np).astype(BF16)
    cs_t = jnp.asarray(cs_np).astype(BF16)

    f = pl.pallas_call(
        functools.partial(_fft_kernel, n1=n1, n2=n2, ng=ng, p1=p1, s2b=s2b, k1b=k1b),
        grid=(b, ngroups // ng, p1 + p2),
        in_specs=[
            pl.BlockSpec((ng, None, n1, s2b, c),
                         lambda bi, gi, p: (gi, bi, 0, jnp.minimum(p, p1 - 1), 0)),
            pl.BlockSpec((s2b, 2 * n1, n1), lambda bi, gi, p: (jnp.minimum(p, p1 - 1), 0, 0)),
            _const_spec((2 * n2, 2 * n2)),
            _const_spec((2 * FNET_GROUP_DIM, FNET_GROUP_DIM)),
        ],
        out_specs=pl.BlockSpec((ng, None, n2, k1b, c),
                               lambda bi, gi, p: (gi, bi, 0, jnp.maximum(p - p1, 0), 0)),
        out_shape=jax.ShapeDtypeStruct((ngroups, b, n2, n1, c), F32),
        scratch_shapes=[pltpu.VMEM((ng, n1 * s2b, c), F32),
                        pltpu.VMEM((ng, n1 * kstride, c), F32),
                        pltpu.VMEM((ng, n2 * k1b, c), F32)],
        compiler_params=pltpu.CompilerParams(
            dimension_semantics=("arbitrary", "arbitrary", "arbitrary"),
            vmem_limit_bytes=VMEM_LIMIT),
        name="fft",
    )(u.reshape(ngroups, b, n1, n2, c), g_t, h_t, cs_t)
    return f.reshape(ngroups, b, s, c)


def _ssm_tables(lam_re, lam_im, log_dt, b_re, b_im, c_re, c_im, d_skip, seg_len):
    assert PAIR == 2
    ndir, ng, p = lam_re.shape
    hd = b_re.shape[-1]
    r = SCAN_BLOCK
    nq = ng // PAIR
    lam_re = lam_re.astype(F32)
    lam_im = lam_im.astype(F32)
    dt = jnp.exp(log_dt.astype(F32))[..., None]

    def apow(k):
        mag = jnp.exp(lam_re * dt * k)
        return mag * jnp.cos(lam_im * dt * k), mag * jnp.sin(lam_im * dt * k)

    ar, ai = apow(1.0)
    den = lam_re * lam_re + lam_im * lam_im
    qr = ((ar - 1.0) * lam_re + ai * lam_im) / den
    qi = (ai * lam_re - (ar - 1.0) * lam_im) / den
    bre = qr[..., None] * b_re.astype(F32) - qi[..., None] * b_im.astype(F32)
    bim = qr[..., None] * b_im.astype(F32) + qi[..., None] * b_re.astype(F32)
    cr = c_re.astype(F32)
    ci = c_im.astype(F32)

    ks = jnp.arange(r + 1, dtype=F32).reshape(r + 1, 1, 1, 1)
    pr, pi = apow(ks)
    abr = pr[:r, ..., None] * bre - pi[:r, ..., None] * bim
    abi = pr[:r, ..., None] * bim + pi[:r, ..., None] * bre

    def in_rows(x):
        return jnp.transpose(x, (1, 0, 3, 2)).reshape(ng, r * hd, p)

    bp_re = jnp.stack([in_rows(abr[::-1, 0]), in_rows(abr[:, 1])])
    bp_im = jnp.stack([in_rows(abi[::-1, 0]), in_rows(abi[:, 1])])

    klag = jnp.sum(cr[None, :, :, :, :, None] * abr[:, :, :, None, :, :]
                   - ci[None, :, :, :, :, None] * abi[:, :, :, None, :, :], axis=4)
    kzero = jnp.zeros_like(klag[0, 0])
    kin = jnp.stack([
        jnp.stack([(klag[i - j, 0] if i >= j else kzero) + (klag[j - i, 1] if j >= i else kzero)
                   for i in range(r)]) for j in range(r)])
    kin = jnp.transpose(kin, (2, 0, 4, 1, 3)).reshape(ng, r * hd, r * hd)
    dsk = d_skip.astype(F32).reshape(ng, hd)
    kin = kin + jnp.eye(r * hd, dtype=F32)[None] * jnp.tile(dsk, (1, r))[:, None, :]

    def state_out(d, pows):
        er = cr[d][None] * pr[pows, d][:, :, None, :] - ci[d][None] * pi[pows, d][:, :, None, :]
        ei = cr[d][None] * pi[pows, d][:, :, None, :] + ci[d][None] * pr[pows, d][:, :, None, :]
        to_rows = lambda e: jnp.transpose(e, (1, 3, 0, 2)).reshape(ng, p, r * hd)
        return to_rows(er), -to_rows(ei)

    cf_re, cf_im = state_out(0, jnp.arange(1, r + 1))
    cb_re, cb_im = state_out(1, r - jnp.arange(r))

    def pair_bd(m):
        lead = m.shape[:-3]
        rr, cc = m.shape[-2:]
        m = m.reshape(lead + (nq, PAIR, rr, cc))
        m0, m1 = m[..., 0, :, :], m[..., 1, :, :]
        z = jnp.zeros_like(m0)
        return jnp.concatenate([jnp.concatenate([m0, z], axis=-1),
                                jnp.concatenate([z, m1], axis=-1)], axis=-2)

    bq = jnp.concatenate([pair_bd(bp_re), pair_bd(bp_im)], axis=-1).astype(BF16)
    wy = jnp.concatenate([pair_bd(kin), pair_bd(cf_re), pair_bd(cf_im),
                          pair_bd(cb_re), pair_bd(cb_im)], axis=-2).astype(BF16)

    def lanes(v):
        v = v.reshape(ndir, nq, 1, PAIR * p)
        return jnp.broadcast_to(v, (ndir, nq, SUBLANES, PAIR * p))

    a8r, a8i = apow(float(r))
    alr, ali = apow(float(seg_len))
    return bq, wy, lanes(a8r), lanes(a8i), lanes(alr), lanes(ali)


def _scan_kernel(*refs, mc, nq, reverse, chain, emit):
    u8_ref, e_ref, bq_ref, a8re_ref, a8im_ref, alre_ref, alim_ref = refs[:7]
    if emit == "y":
        xo_ref, wy_ref, y8_ref, eo_ref, d_scr, st_scr = refs[7:]
    elif emit == "x":
        x_ref, eo_ref, d_scr, st_scr = refs[7:]
    else:
        eo_ref, d_scr, st_scr = refs[7:]
    i = pl.program_id(0)
    n = pl.num_programs(0)
    rows = mc * SUBLANES
    hl = LANES

    @pl.when(i == 0)
    def _init():
        if not chain:
            st_scr[...] = jnp.zeros(st_scr.shape, F32)
        else:
            row = lax.broadcasted_iota(jnp.int32, (SUBLANES, hl), 0)
            shift = SUBLANES - 1 if reverse else 1
            keep = (row <= SUBLANES - 2) if reverse else (row >= 1)
            for q in range(nq):
                er = pltpu.roll(e_ref[q, :, 0:hl], shift, 0)
                ei = pltpu.roll(e_ref[q, :, hl:2 * hl], shift, 0)
                lr = alre_ref[q]
                li = alim_ref[q]
                xr = jnp.zeros((SUBLANES, hl), F32)
                xi = jnp.zeros((SUBLANES, hl), F32)
                for _ in range(SUBLANES - 1):
                    pr = pltpu.roll(xr, shift, 0)
                    pi = pltpu.roll(xi, shift, 0)
                    xr = jnp.where(keep, lr * pr - li * pi + er, 0.0)
                    xi = jnp.where(keep, lr * pi + li * pr + ei, 0.0)
                st_scr[q, :, 0:hl] = xr
                st_scr[q, :, hl:2 * hl] = xi

    for q0 in range(0, nq, SCAN_INTERLEAVE):
        qs = list(range(q0, min(q0 + SCAN_INTERLEAVE, nq)))
        for k, q in enumerate(qs):
            d_scr[k] = _dot(u8_ref[q], bq_ref[q])
        a_re = [a8re_ref[q] for q in qs]
        a_im = [a8im_ref[q] for q in qs]

        def body(step, carry, a_re=a_re, a_im=a_im, nk=len(qs)):
            m = (mc - 1 - step) if reverse else step
            r = pl.multiple_of(m * SUBLANES, SUBLANES)
            out = []
            for k in range(nk):
                re, im = carry[2 * k], carry[2 * k + 1]
                d_re = d_scr[k, pl.ds(r, SUBLANES), 0:hl]
                d_im = d_scr[k, pl.ds(r, SUBLANES), hl:2 * hl]
                d_scr[k, pl.ds(r, SUBLANES), 0:hl] = re
                d_scr[k, pl.ds(r, SUBLANES), hl:2 * hl] = im
                out.append(a_re[k] * re - a_im[k] * im + d_re)
                out.append(a_re[k] * im + a_im[k] * re + d_im)
            return tuple(out)

        init = []
        for q in qs:
            init += [st_scr[q, :, 0:hl], st_scr[q, :, hl:2 * hl]]
        fin = lax.fori_loop(0, mc, body, tuple(init), unroll=2)
        for k, q in enumerate(qs):
            st_scr[q, :, 0:hl] = fin[2 * k]
            st_scr[q, :, hl:2 * hl] = fin[2 * k + 1]
            if emit == "x":
                x_ref[q] = d_scr[k].astype(BF16)
            elif emit == "y":
                lhs = jnp.concatenate([u8_ref[q], xo_ref[q], d_scr[k].astype(BF16)], axis=1)
                y8_ref[q] = _dot(lhs, wy_ref[q]).astype(BF16)

    @pl.when(i == n - 1)
    def _fin():
        eo_ref[...] = st_scr[...]


def _scan(u8, e_in, bq, a8re, a8im, alre, alim, mc, reverse, chain, emit, x_other=None, wy=None):
    nq = bq.shape[0]
    mtot = u8.shape[1] // NB
    n = mtot // mc
    rows = mc * NB
    order = (lambda i: n - 1 - i) if reverse else (lambda i: i)
    st_shape = (nq, SUBLANES, 2 * LANES)
    row_spec = pl.BlockSpec((nq, rows, 2 * LANES), lambda i: (0, order(i), 0))
    row_shape = jax.ShapeDtypeStruct((nq, mtot * NB, 2 * LANES), BF16)
    in_specs = [row_spec, _const_spec(st_shape), _const_spec(bq.shape),
                _const_spec(a8re.shape), _const_spec(a8im.shape),
                _const_spec(alre.shape), _const_spec(alim.shape)]
    args = [u8, e_in, bq, a8re, a8im, alre, alim]
    out_specs = [_const_spec(st_shape)]
    out_shape = [jax.ShapeDtypeStruct(st_shape, F32)]
    if emit == "y":
        in_specs += [row_spec, _const_spec(wy.shape)]
        args += [x_other, wy]
    if emit is not None:
        out_specs = [row_spec] + out_specs
        out_shape = [row_shape] + out_shape
    kern = functools.partial(_scan_kernel, mc=mc, nq=nq, reverse=reverse, chain=chain, emit=emit)
    return pl.pallas_call(
        kern,
        grid=(n,),
        in_specs=in_specs,
        out_specs=out_specs,
        out_shape=out_shape,
        scratch_shapes=[pltpu.VMEM((SCAN_INTERLEAVE, rows, 2 * LANES), F32),
                        pltpu.VMEM(st_shape, F32)],
        compiler_params=pltpu.CompilerParams(
            dimension_semantics=("arbitrary",), vmem_limit_bytes=VMEM_LIMIT),
        name="ssm_scan_" + ("bwd" if reverse else "fwd") + ("_chain" if chain else "")
             + ("_" + emit if emit else "_end"),
    )(*args)


def _ssm(u8, tables, chained, mc):
    bq, wy, a8re, a8im, alre, alim = tables
    nq = bq.shape[1]
    ef = eb = jnp.zeros((nq, SUBLANES, 2 * LANES), F32)
    par = lambda dr: (bq[dr], a8re[dr], a8im[dr], alre[dr], alim[dr])
    if chained:
        (ef,) = _scan(u8, ef, *par(0), mc, False, False, None)
        (eb,) = _scan(u8, eb, *par(1), mc, True, False, None)
    xf, _ = _scan(u8, ef, *par(0), mc, False, chained, "x")
    y8, _ = _scan(u8, eb, *par(1), mc, True, chained, "y", xf, wy)
    return y8


def _ffn_chunks(hidden, step=1024):
    edges = list(range(0, hidden, step)) + [hidden]
    return list(zip(edges[:-1], edges[1:]))


def _mix_kernel(x_ref, f_ref, y8_ref, gates_ref, gpost_ref,
                wfo_ref, wval_ref, wgate_ref, wout_ref, o_ref, y_scr, *, tt, d, fw, sw):
    rows = NB * tt
    qpb = GROUPS_PER_BLOCK // PAIR
    for cb in range(sw // LANES):
        vs = [y8_ref[cb * qpb + gl // PAIR, :, (gl % PAIR) * LANES:(gl % PAIR + 1) * LANES].astype(F32)
              for gl in range(GROUPS_PER_BLOCK)]
        unfolded = _transpose_pieces(vs)
        for i in range(SCAN_BLOCK):
            for m in range(tt // SCAN_BLOCK):
                y_scr[cb, pl.ds((SCAN_BLOCK * m + i) * NB, NB), :] = unfolded[i][m * NB:(m + 1) * NB, :]
    ys = jnp.concatenate(
        [jnp.concatenate(
            [y_scr[cb, pl.ds(b, tt, stride=NB), :] for b in range(NB)], axis=0)
         for cb in range(sw // LANES)], axis=1)
    z = jax.nn.gelu(ys).astype(BF16)
    f = jnp.concatenate(
        [f_ref[g].reshape(rows, LANES) for g in range(fw // LANES)], axis=1).astype(BF16)
    br_a = _dot(f, wfo_ref[...])
    br_b = _dot(z, wval_ref[...]) * _sigmoid(_dot(z, wgate_ref[...]))
    sa = gates_ref[:, :, 0:d].reshape(rows, d).astype(F32)
    sb = gates_ref[:, :, d:2 * d].reshape(rows, d).astype(F32)
    merged = (sa * br_a + sb * br_b).astype(BF16)
    m = _dot(merged, wout_ref[...])
    o_ref[...] = (x_ref[...].reshape(rows, d) + _rms(m, gpost_ref[...])).reshape(NB, tt, d)


def _mix(xv, f, y8, gates, gpost, wfo, wval, wgate, wout, tt):
    nb, sseg, d = xv.shape
    fw = f.shape[0] * LANES
    nq = y8.shape[0]
    sw = nq * PAIR * SSM_GROUP_DIM
    n = sseg // tt
    assert tt % SCAN_BLOCK == 0
    kern = functools.partial(_mix_kernel, tt=tt, d=d, fw=fw, sw=sw)
    tok = lambda width: pl.BlockSpec((NB, tt, width), lambda i: (0, i, 0))
    tmaj = pl.BlockSpec((nq, tt, PAIR * LANES), lambda i: (0, i, 0))
    return pl.pallas_call(
        kern,
        grid=(n,),
        in_specs=[tok(d), pl.BlockSpec((fw // LANES, NB, tt, LANES), lambda i: (0, 0, i, 0)),
                  tmaj, tok(2 * d), _const_spec((1, d)),
                  _const_spec(wfo.shape), _const_spec(wval.shape), _const_spec(wgate.shape),
                  _const_spec(wout.shape)],
        out_specs=tok(d),
        out_shape=jax.ShapeDtypeStruct((NB, sseg, d), F32),
        scratch_shapes=[pltpu.VMEM((sw // LANES, NB * tt, LANES), F32)],
        compiler_params=pltpu.CompilerParams(
            dimension_semantics=("arbitrary",), vmem_limit_bytes=VMEM_LIMIT),
        name="mix",
    )(xv, f, y8, gates, gpost, wfo, wval, wgate, wout)


def _ffn_kernel(x_ref, gpre_ref, gpost_ref, wg_ref, wu_ref, wd_ref, o_ref, *, tt, d, hidden):
    rows = NB * tt
    x1 = x_ref[...].reshape(rows, d)
    h2 = _rms(x1, gpre_ref[...]).astype(BF16)
    acc = None
    for c0, c1 in _ffn_chunks(hidden):
        g = _dot(h2, wg_ref[:, c0:c1])
        u = _dot(h2, wu_ref[:, c0:c1])
        part = _dot((g * _sigmoid(g) * u).astype(BF16), wd_ref[c0:c1, :])
        acc = part if acc is None else acc + part
    o_ref[...] = (x1 + _rms(acc, gpost_ref[...])).reshape(NB, tt, d)


def _ffn(xv, gpre, gpost, wg, wu, wd, tt):
    nb, sseg, d = xv.shape
    hidden = wg.shape[-1]
    tok = pl.BlockSpec((NB, tt, d), lambda i: (0, i, 0))
    return pl.pallas_call(
        functools.partial(_ffn_kernel, tt=tt, d=d, hidden=hidden),
        grid=(sseg // tt,),
        in_specs=[tok, _const_spec((1, d)), _const_spec((1, d)),
                  _const_spec(wg.shape), _const_spec(wu.shape), _const_spec(wd.shape)],
        out_specs=tok,
        out_shape=jax.ShapeDtypeStruct((NB, sseg, d), F32),
        compiler_params=pltpu.CompilerParams(
            dimension_semantics=("arbitrary",), vmem_limit_bytes=VMEM_LIMIT),
        name="ffn",
    )(xv, gpre, gpost, wg, wu, wd)


def _pick(total, want):
    t = min(want, total)
    assert total % t == 0, (total, t)
    return t


def _encoder_layer(x, p):
    b, s, d = x.shape
    assert b in (1, NB), "one sequence (split in NB segments) or NB sequences"
    sseg = (b * s) // NB
    fw = p["w_fnet_out"].shape[0]
    sw = p["w_glu_val"].shape[0]
    xv = x.reshape(NB, sseg, d)
    tt = _pick(sseg, TOKEN_TILE_T)

    uf, u8, gates = _in_proj(xv, p["norm_mix_pre"], p["w_in"], fw, sw, tt)
    ng = fw // LANES
    f = _fourier_mix(uf.reshape(ng, b, s, LANES)).reshape(ng, NB, sseg, LANES)

    tables = _ssm_tables(p["lam_re"], p["lam_im"], p["log_dt"], p["b_re"], p["b_im"],
                         p["c_re"], p["c_im"], p["d_skip"], sseg)
    ys = _ssm(u8, tables, chained=(b == 1), mc=_pick(sseg // SCAN_BLOCK, SCAN_CHUNK_BLOCKS))

    x1 = _mix(xv, f, ys, gates, p["norm_mix_post"],
              p["w_fnet_out"], p["w_glu_val"], p["w_glu_gate"], p["w_out"], tt)
    y = _ffn(x1, p["norm_ffn_pre"], p["norm_ffn_post"],
             p["w_ffn_gate"], p["w_ffn_up"], p["w_ffn_down"], tt)
    return y.reshape(b, s, d)


def _layer_params(l, norm_mix_pre, norm_mix_post, norm_ffn_pre, norm_ffn_post, w_in, w_fnet_out,
                  lam_re, lam_im, log_dt, b_re, b_im, c_re, c_im, d_skip, w_glu_val, w_glu_gate,
                  w_out, w_ffn_gate, w_ffn_up, w_ffn_down):
    row = lambda v: v[l].astype(F32).reshape(1, -1)
    w = lambda v: v[l].astype(BF16)
    return dict(
        norm_mix_pre=row(norm_mix_pre), norm_mix_post=row(norm_mix_post),
        norm_ffn_pre=row(norm_ffn_pre), norm_ffn_post=row(norm_ffn_post),
        w_in=w(w_in), w_fnet_out=w(w_fnet_out),
        lam_re=lam_re[l], lam_im=lam_im[l], log_dt=log_dt[l],
        b_re=b_re[l], b_im=b_im[l], c_re=c_re[l], c_im=c_im[l], d_skip=row(d_skip),
        w_glu_val=w(w_glu_val), w_glu_gate=w(w_glu_gate), w_out=w(w_out),
        w_ffn_gate=w(w_ffn_gate), w_ffn_up=w(w_ffn_up), w_ffn_down=w(w_ffn_down))


def kernel(x_prompt, x_sample, norm_mix_pre, norm_mix_post, norm_ffn_pre, norm_ffn_post, w_in, w_fnet_out, lam_re, lam_im, log_dt, b_re, b_im, c_re, c_im, d_skip, w_glu_val, w_glu_gate, w_out, w_ffn_gate, w_ffn_up, w_ffn_down):
    weights = (norm_mix_pre, norm_mix_post, norm_ffn_pre, norm_ffn_post, w_in, w_fnet_out,
               lam_re, lam_im, log_dt, b_re, b_im, c_re, c_im, d_skip, w_glu_val, w_glu_gate,
               w_out, w_ffn_gate, w_ffn_up, w_ffn_down)
    depth = w_in.shape[0]
    outs = []
    for x in (x_prompt, x_sample):
        for l in range(depth):
            x = _encoder_layer(x, _layer_params(l, *weights))
        outs.append(x)
    return tuple(outs)
```
